```python
import math
import jax, jax.numpy as jnp
from jax import lax
import numpy as np

D_MODEL = 2048
BATCH = 1
SEQ = 16384
DEPTH = 2

GRID_W = 64
CTX_LEN = 256
N_MOD = 6
EPS = 1e-6
HY_D = D_MODEL // 2
HY_HEADS = 8
HY_ORDER = 2
HY_EMB = 33
HY_FILTER_W = 64
HY_DECAY_TARGET = 1e-2
HY_FAST_PCT = 0.3
HY_SLOW_PCT = 1.5
HG_D = D_MODEL // 2
HG_EXPAND = 128
HG_HEADS = HG_D // HG_EXPAND
HG_CHUNK = 64
HG_Q0 = 3 * HY_D
HG_I0 = HG_Q0 + HG_D
IN_COLS = 3 * HY_D + 5 * HG_D
POOL_WINDOWS = (2, 4, 8, 16)
POOL_D = D_MODEL // len(POOL_WINDOWS)
D_FF = 5632

kernel_name = "hybrid_hyena_hgrn2_pool_convffn_dit"


def rmsnorm(x, g):
    xf = x.astype(jnp.float32)
    y = xf * lax.rsqrt(jnp.mean(xf * xf, axis=-1, keepdims=True) + EPS)
    return (y * g.astype(jnp.float32)).astype(x.dtype)


def head_rmsnorm(x, g, heads):
    B, L, C = x.shape
    xf = x.astype(jnp.float32).reshape(B, L, heads, C // heads)
    xf = xf * lax.rsqrt(jnp.mean(xf * xf, axis=-1, keepdims=True) + EPS)
    return (xf.reshape(B, L, C) * g).astype(x.dtype)


def adaln(cvec, w, b):
    return jnp.split((jax.nn.silu(cvec) @ w + b)[:, None, :], N_MOD, axis=-1)


def dwconv1d(x, w, b):
    L = x.shape[1]
    xp = jnp.pad(x, ((0, 0), (1, 1), (0, 0)))
    return xp[:, :L] * w[0] + xp[:, 1:L + 1] * w[1] + xp[:, 2:] * w[2] + b


def dwconv2d_grid(x, w, b):
    B, L, C = x.shape
    rows = L // GRID_W
    xg = jnp.pad(x.reshape(B, rows, GRID_W, C), ((0, 0), (1, 1), (1, 1), (0, 0)))
    y = b
    for di in range(3):
        for dj in range(3):
            y = y + xg[:, di:di + rows, dj:dj + GRID_W] * w[di, dj]
    return y.reshape(B, L, C)


def hyena_filters(L, w1, b1, w2, b2, w3, b3, freq, w4):
    t = jnp.linspace(0.0, 1.0, L, dtype=jnp.float32)[:, None]
    bands = (HY_EMB - 1) // 2
    f = jnp.linspace(1e-4, bands - 1, bands, dtype=jnp.float32)[None, :]
    w = (2.0 * math.pi / L) * jnp.arange(L, dtype=jnp.float32)[:, None]
    z = jnp.concatenate([t, jnp.cos(f * w), -jnp.sin(f * w)], axis=-1)
    h = jnp.sin(freq[0] * (z @ w1 + b1))
    h = jnp.sin(freq[1] * (h @ w2 + b2))
    h = jnp.sin(freq[2] * (h @ w3 + b3))
    h = (h @ w4).astype(jnp.float32).reshape(L, HY_ORDER, 2, HY_D)
    max_decay = math.log(HY_DECAY_TARGET) / HY_FAST_PCT
    min_decay = math.log(HY_DECAY_TARGET) / HY_SLOW_PCT
    deltas = jnp.abs(jnp.linspace(min_decay, max_decay, HY_D, dtype=jnp.float32))
    return h * jnp.exp(-t[:, :, None, None] * deltas)


def long_conv_bidir(u, h_fwd, h_bwd, skip):
    L = u.shape[1]
    kern = jnp.concatenate([h_fwd, jnp.zeros_like(h_fwd[:1]), h_bwd[:0:-1]], axis=0)
    uf = jnp.fft.rfft(u.astype(jnp.float32), n=2 * L, axis=1)
    kf = jnp.fft.rfft(kern.astype(jnp.float32), axis=0)
    y = jnp.fft.irfft(uf * kf[None], n=2 * L, axis=1)[:, :L]
    return (y + u * skip).astype(u.dtype)


def hyena_mixer(proj, conv_w, conv_b, filt, skip, norm_g):
    L = proj.shape[1]
    v, x1, x2 = jnp.split(dwconv1d(proj, conv_w, conv_b), 3, axis=-1)
    h = hyena_filters(L, *filt)
    z = v
    for n, gate in enumerate((x1, x2)):
        z = gate * long_conv_bidir(z, h[:, n, 0], h[:, n, 1], skip[n])
    return head_rmsnorm(z, norm_g, HY_HEADS)


def to_heads(a):
    B, L, _ = a.shape
    return a.reshape(B, L, HG_HEADS, HG_EXPAND).transpose(0, 2, 1, 3)


def hgrn_gates(f_logit, lb):
    f = lb + (1.0 - lb) * jax.nn.sigmoid(f_logit.astype(jnp.float32))
    return jnp.log(f), 1.0 - f


def gla_chunked(q, k, v, logf, s0):
    B, H, L, K = q.shape
    V = v.shape[-1]
    n = L // HG_CHUNK
    r = lambda a: a.reshape(B, H, n, HG_CHUNK, a.shape[-1])
    q, k, v, logf = r(q), r(k), r(v), r(logf)
    b = jnp.cumsum(logf, axis=3)
    b_last = b[:, :, :, -1:]
    q_in = q * jnp.exp(b)
    k_intra = k * jnp.exp(-b)
    k_state = k * jnp.exp(b_last - b)
    tri = jnp.tril(jnp.ones((HG_CHUNK, HG_CHUNK), dtype=bool))
    a = jnp.where(tri, jnp.einsum('bhntk,bhnsk->bhnts', q_in, k_intra), 0.0)
    o = jnp.einsum('bhnts,bhnsv->bhntv', a, v)
    ds = jnp.einsum('bhnsk,bhnsv->bhnkv', k_state, v)
    decay = jnp.exp(b_last[:, :, :, 0])

    def step(s, inp):
        d, u = inp
        return d[..., None] * s + u, s

    _, s_start = lax.scan(step, s0.astype(jnp.float32),
                          (jnp.moveaxis(decay, 2, 0), jnp.moveaxis(ds, 2, 0)))
    s_start = jnp.moveaxis(s_start, 0, 2)
    o = o + jnp.einsum('bhntk,bhnkv->bhntv', q_in, s_start)
    return o.reshape(B, H, L, V)


def gla_final_state(k, v, logf):
    b = jnp.cumsum(logf, axis=2)
    return jnp.einsum('bhsk,bhsv->bhkv', k * jnp.exp(b[:, :, -1:] - b), v)


def hgrn_context_states(i, f_fw, f_bw, lb):
    lbh = lb.reshape(HG_HEADS, 1, HG_EXPAND)
    vh = to_heads(i)
    logf_fw, k_fw = hgrn_gates(to_heads(f_fw), lbh)
    logf_bw, k_bw = hgrn_gates(to_heads(f_bw), lbh)
    flip = lambda a: jnp.flip(a, axis=2)
    return (gla_final_state(k_fw, vh, logf_fw),
            gla_final_state(flip(k_bw), flip(vh), flip(logf_bw)))


def hgrn_mixer(q, i, f_fw, f_bw, g, lb, norm_g, s0_fw, s0_bw):
    B, L, _ = q.shape
    lbh = lb.reshape(HG_HEADS, 1, HG_EXPAND)
    qh, vh = to_heads(jax.nn.silu(q)), to_heads(i)
    logf_fw, k_fw = hgrn_gates(to_heads(f_fw), lbh)
    logf_bw, k_bw = hgrn_gates(to_heads(f_bw), lbh)
    flip = lambda a: jnp.flip(a, axis=2)
    o_fw = gla_chunked(qh, k_fw, vh, logf_fw, s0_fw)
    o_bw = flip(gla_chunked(flip(qh), flip(k_bw), flip(vh), flip(logf_bw), s0_bw))
    o = (o_fw + o_bw).transpose(0, 2, 1, 3)
    o = o * lax.rsqrt(jnp.mean(o * o, axis=-1, keepdims=True) + EPS)
    return (o.reshape(B, L, HG_D) * norm_g * jax.nn.silu(g.astype(jnp.float32))).astype(q.dtype)


def pool_mixer(h, w, b, scale):
    B, L, D = h.shape
    hf = h.astype(jnp.float32)
    cs = jnp.pad(jnp.cumsum(hf, axis=1), ((0, 0), (1, 0), (0, 0)))
    t = jnp.arange(L)
    outs = []
    for gi, win in enumerate(POOL_WINDOWS):
        lo = jnp.clip(t - win // 2, 0, L)
        hi = jnp.clip(t + win // 2, 0, L)
        sl = slice(gi * POOL_D, (gi + 1) * POOL_D)
        csg = cs[:, :, sl]
        mean = (csg[:, hi] - csg[:, lo]) / (hi - lo).astype(jnp.float32)[:, None]
        outs.append(jnp.einsum('bld,de->ble', (mean - hf[:, :, sl]).astype(h.dtype), w[gi]) + b[gi])
    return jnp.concatenate(outs, axis=-1) * scale


def conv_ffn(h, w_up, conv_w, conv_b, w_down):
    a, u = jnp.split(h @ w_up, 2, axis=-1)
    a = dwconv2d_grid(a, conv_w, conv_b)
    return (jax.nn.gelu(a, approximate=False) * u) @ w_down


def setup_inputs(seed: int = 0) -> dict:
    key = jax.random.key(seed)
    keys = iter(jax.random.split(key, 64))
    n_even = (DEPTH + 1) // 2
    n_odd = DEPTH // 2

    def dense(shape, fan_in, s=1.0):
        return jax.random.normal(next(keys), shape, jnp.float32) * (s * fan_in ** -0.5)

    def gain(shape):
        return 1.0 + 0.02 * jax.random.normal(next(keys), shape, jnp.float32)

    def small(shape):
        return 0.01 * jax.random.normal(next(keys), shape, jnp.float32)

    return {
        "x": jax.random.normal(next(keys), (BATCH, SEQ, D_MODEL), jnp.float32),
        "c": jax.random.normal(next(keys), (BATCH, D_MODEL), jnp.float32),
        "ctx": jax.random.normal(next(keys), (BATCH, CTX_LEN, D_MODEL), jnp.float32),
        "c_ctx": jax.random.normal(next(keys), (D_MODEL,), jnp.float32),
        "norm_mix_g": gain((DEPTH, D_MODEL)),
        "norm_ffn_g": gain((DEPTH, D_MODEL)),
        "mod_w": dense((DEPTH, D_MODEL, N_MOD * D_MODEL), D_MODEL, 0.5),
        "mod_b": small((DEPTH, N_MOD * D_MODEL)),
        "in_w": dense((n_even, D_MODEL, IN_COLS), D_MODEL),
        "in_b": small((n_even, IN_COLS)),
        "hy_conv_w": dense((n_even, 3, 3 * HY_D), 3),
        "hy_conv_b": small((n_even, 3 * HY_D)),
        "hy_w1": dense((n_even, HY_EMB, HY_FILTER_W), HY_EMB),
        "hy_b1": small((n_even, HY_FILTER_W)),
        "hy_w2": dense((n_even, HY_FILTER_W, HY_FILTER_W), HY_FILTER_W),
        "hy_b2": small((n_even, HY_FILTER_W)),
        "hy_w3": dense((n_even, HY_FILTER_W, HY_FILTER_W), HY_FILTER_W),
        "hy_b3": small((n_even, HY_FILTER_W)),
        "hy_freq": gain((n_even, 3, HY_FILTER_W)),
        "hy_w4": dense((n_even, HY_FILTER_W, HY_ORDER * 2 * HY_D), HY_FILTER_W),
        "hy_skip": dense((n_even, HY_ORDER, HY_D), 1),
        "hy_norm_g": gain((n_even, HY_D)),
        "hg_lb_logits": gain((n_even + 1, HG_D)),
        "hg_norm_g": gain((n_even, HG_D)),
        "out_w": dense((n_even, D_MODEL, D_MODEL), D_MODEL),
        "pool_w": dense((n_odd, len(POOL_WINDOWS), POOL_D, POOL_D), POOL_D),
        "pool_b": small((n_odd, len(POOL_WINDOWS), POOL_D)),
        "pool_scale": gain((n_odd, D_MODEL)),
        "ffn_up_w": dense((DEPTH, D_MODEL, 2 * D_FF), D_MODEL),
        "ffn_conv_w": dense((DEPTH, 3, 3, D_FF), 9),
        "ffn_conv_b": small((DEPTH, D_FF)),
        "ffn_down_w": dense((DEPTH, D_FF, D_MODEL), D_FF),
        "final_norm_g": gain((D_MODEL,)),
    }


def reference(x, c, ctx, c_ctx, norm_mix_g, norm_ffn_g, mod_w, mod_b, in_w, in_b, hy_conv_w, hy_conv_b,
              hy_w1, hy_b1, hy_w2, hy_b2, hy_w3, hy_b3, hy_freq, hy_w4, hy_skip, hy_norm_g,
              hg_lb_logits, hg_norm_g, out_w, pool_w, pool_b, pool_scale,
              ffn_up_w, ffn_conv_w, ffn_conv_b, ffn_down_w, final_norm_g):
    lbs = jnp.cumsum(jax.nn.softmax(hg_lb_logits.astype(jnp.float32), axis=0), axis=0)
    for l in range(DEPTH):
        sh1, sc1, g1, sh2, sc2, g2 = adaln(c, mod_w[l], mod_b[l])
        h = rmsnorm(x, norm_mix_g[l]) * (1.0 + sc1) + sh1
        if l % 2 == 0:
            e = l // 2
            sh_c, sc_c = adaln(c_ctx[None], mod_w[l], mod_b[l])[:2]
            hc = rmsnorm(ctx, norm_mix_g[l]) * (1.0 + sc_c) + sh_c
            pc = hc @ in_w[e, :, HG_I0:HG_I0 + 3 * HG_D] + in_b[e, HG_I0:HG_I0 + 3 * HG_D]
            s0_fw, s0_bw = hgrn_context_states(*jnp.split(pc, 3, axis=-1), lbs[e])
            proj = h @ in_w[e] + in_b[e]
            filt = (hy_w1[e], hy_b1[e], hy_w2[e], hy_b2[e], hy_w3[e], hy_b3[e], hy_freq[e], hy_w4[e])
            y_hy = hyena_mixer(proj[..., :HG_Q0], hy_conv_w[e], hy_conv_b[e], filt, hy_skip[e], hy_norm_g[e])
            q, i, f_fw, f_bw, g = jnp.split(proj[..., HG_Q0:], 5, axis=-1)
            y_hg = hgrn_mixer(q, i, f_fw, f_bw, g, lbs[e], hg_norm_g[e], s0_fw, s0_bw)
            y = jnp.concatenate([y_hy, y_hg], axis=-1) @ out_w[e]
        else:
            od = l // 2
            y = pool_mixer(h, pool_w[od], pool_b[od], pool_scale[od])
        x = x + g1 * y
        h = rmsnorm(x, norm_ffn_g[l]) * (1.0 + sc2) + sh2
        x = x + g2 * conv_ffn(h, ffn_up_w[l], ffn_conv_w[l], ffn_conv_b[l], ffn_down_w[l])
    return rmsnorm(x, final_norm_g)
```

```python
import functools
import math

import numpy as np
import jax
import jax.numpy as jnp
from jax import lax
from jax.experimental import pallas as pl
from jax.experimental.pallas import tpu as pltpu

F32 = jnp.float32
BF16 = jnp.bfloat16
EPS = 1e-6

LANES = 128
SUBLANES = 8
VMEM_LIMIT_BYTES = 56 * 1024 * 1024

GRID_W = 64
N_MOD = 6
HY_HEADS = 8
HY_ORDER = 2
HY_EMB = 33
HY_DECAY_TARGET = 1e-2
HY_FAST_PCT = 0.3
HY_SLOW_PCT = 1.5
HG_EXPAND = 128
HG_CHUNK = 64
POOL_WINDOWS = (2, 4, 8, 16)
POOL_HALO = 8
FFT_B = 128


def _params(*sem):
    return pltpu.CompilerParams(dimension_semantics=sem, vmem_limit_bytes=VMEM_LIMIT_BYTES)


def _sds(shape, dtype):
    return jax.ShapeDtypeStruct(shape, dtype)


def _dot(a, b):
    return jnp.dot(a, b, preferred_element_type=F32)


def _adaln_kernel(c_ref, w_ref, b_ref, o_ref):
    c = c_ref[...]
    s = (c * jax.nn.sigmoid(c)).astype(BF16)
    o_ref[...] = _dot(s, w_ref[...].astype(BF16)) + b_ref[...]


def _adaln(cvecs, mod_w, mod_b, layer, tn=1024):
    _, d, n = mod_w.shape
    tn = min(tn, n)
    return pl.pallas_call(
        _adaln_kernel,
        grid=(n // tn,),
        in_specs=[
            pl.BlockSpec((SUBLANES, d), lambda j: (0, 0)),
            pl.BlockSpec((None, d, tn), lambda j: (layer, 0, j)),
            pl.BlockSpec((None, 1, tn), lambda j: (layer, 0, j)),
        ],
        out_specs=pl.BlockSpec((SUBLANES, tn), lambda j: (0, j)),
        out_shape=_sds((SUBLANES, n), F32),
        compiler_params=_params("arbitrary"),
        name="adaln",
    )(cvecs, mod_w, mod_b.reshape(mod_b.shape[0], 1, n))


def _rms_mod(x, g, sc, sh):
    y = x * lax.rsqrt(jnp.mean(x * x, axis=-1, keepdims=True) + EPS) * g
    return y * (1.0 + sc) + sh


def _norm_mod_kernel(x_ref, g_ref, sc_ref, sh_ref, o_ref):
    o_ref[...] = _rms_mod(x_ref[...], g_ref[...], sc_ref[...], sh_ref[...]).astype(o_ref.dtype)


def _norm_mod(x, g, sc, sh, tm=512):
    m, d = x.shape
    tm = min(tm, m)
    row = pl.BlockSpec((1, d), lambda i: (0, 0))
    return pl.pallas_call(
        _norm_mod_kernel,
        grid=(m // tm,),
        in_specs=[pl.BlockSpec((tm, d), lambda i: (i, 0)), row, row, row],
        out_specs=pl.BlockSpec((tm, d), lambda i: (i, 0)),
        out_shape=_sds((m, d), BF16),
        compiler_params=_params("parallel"),
        name="norm_mod",
    )(x, g.reshape(1, d), sc, sh)


def _matmul_bias_kernel(x_ref, w_ref, b_ref, o_ref):
    o_ref[...] = _dot(x_ref[...], w_ref[...]) + b_ref[...]


def _matmul_bias(x, w, b, tm=1024, tn=1024):
    m, k = x.shape
    n = w.shape[1]
    tm, tn = min(tm, m), min(tn, n)
    return pl.pallas_call(
        _matmul_bias_kernel,
        grid=(n // tn, m // tm),
        in_specs=[
            pl.BlockSpec((tm, k), lambda j, i: (i, 0)),
            pl.BlockSpec((k, tn), lambda j, i: (0, j)),
            pl.BlockSpec((1, tn), lambda j, i: (0, j)),
        ],
        out_specs=pl.BlockSpec((tm, tn), lambda j, i: (i, j)),
        out_shape=_sds((m, n), F32),
        compiler_params=_params("parallel", "parallel"),
        name="matmul_bias",
    )(x, w, b.reshape(1, n))


def _dwconv1d_kernel(xm_ref, xp_ref, xn_ref, w_ref, b_ref, o_ref):
    i = pl.program_id(0)
    x = xm_ref[...]
    tm = x.shape[0]
    prev_row = jnp.where(i > 0, xp_ref[SUBLANES - 1:SUBLANES, :], 0.0)
    next_row = jnp.where(i < pl.num_programs(0) - 1, xn_ref[0:1, :], 0.0)
    row = lax.broadcasted_iota(jnp.int32, x.shape, 0)
    x_m1 = jnp.where(row == 0, prev_row, pltpu.roll(x, 1, 0))
    x_p1 = jnp.where(row == tm - 1, next_row, pltpu.roll(x, tm - 1, 0))
    o_ref[...] = x_m1 * w_ref[0:1, :] + x * w_ref[1:2, :] + x_p1 * w_ref[2:3, :] + b_ref[...]


def _dwconv1d(proj, w, b, parts, c, tm=512):
    l = proj.shape[0]
    tm = min(tm, l)
    r = tm // SUBLANES
    last = l // SUBLANES - 1
    return pl.pallas_call(
        _dwconv1d_kernel,
        grid=(l // tm, parts),
        in_specs=[
            pl.BlockSpec((tm, c), lambda i, j: (i, j)),
            pl.BlockSpec((SUBLANES, c), lambda i, j: (jnp.maximum(i * r - 1, 0), j)),
            pl.BlockSpec((SUBLANES, c), lambda i, j: (jnp.minimum((i + 1) * r, last), j)),
            pl.BlockSpec((3, c), lambda i, j: (0, j)),
            pl.BlockSpec((1, c), lambda i, j: (0, j)),
        ],
        out_specs=pl.BlockSpec((None, tm, c), lambda i, j: (j, i, 0)),
        out_shape=_sds((parts, l, c), F32),
        compiler_params=_params("parallel", "parallel"),
        name="hyena_dwconv1d",
    )(proj, proj, proj, w, b.reshape(1, parts * c))


def _hyena_pos_features(l):
    t = jnp.linspace(0.0, 1.0, l, dtype=F32)[:, None]
    bands = (HY_EMB - 1) // 2
    f = jnp.linspace(1e-4, bands - 1, bands, dtype=F32)[None, :]
    w = (2.0 * math.pi / l) * jnp.arange(l, dtype=F32)[:, None]
    return jnp.concatenate([t, jnp.cos(f * w), -jnp.sin(f * w)], axis=-1)


def _dft_constants(l):
    b = FFT_B
    a = l // b
    na = 2 * a
    n = 2 * l
    ang1 = 2.0 * np.pi * np.outer(np.arange(na), np.arange(na)) / na
    f1 = np.concatenate([np.cos(ang1), -np.sin(ang1)], axis=0)
    angb = 2.0 * np.pi * np.outer(np.arange(b), np.arange(b)) / b
    cb, sb = np.cos(angb), np.sin(angb)
    m_fwd = np.block([[cb, sb], [-sb, cb]])
    m_inv = np.block([[cb, -sb], [sb, cb]])
    f3 = np.concatenate([np.cos(ang1[:a]), -np.sin(ang1[:a])], axis=1) / n
    as_bf16 = lambda m: jnp.asarray(m, dtype=F32).astype(BF16)
    return as_bf16(f1), as_bf16(m_fwd), as_bf16(m_inv), as_bf16(f3)


def _twiddle_tables(l):
    b = FFT_B
    na = 2 * (l // b)
    n = 2 * l
    kap = jnp.arange(na, dtype=jnp.int32)
    bi = jnp.arange(b, dtype=jnp.int32)
    ang = ((bi[:, None] * kap[None, :]) % n).astype(F32) * (2.0 * math.pi / n)
    cos, sin = jnp.cos(ang), jnp.sin(ang)
    rep = lambda t: jnp.broadcast_to(t[:, :, None], t.shape + (LANES,))
    tw_fwd = rep(jnp.concatenate([cos, -sin], axis=1))
    tw_inv = rep(jnp.concatenate([cos.T, sin.T], axis=1))
    return tw_fwd, tw_inv


def _cmul(ar, ai, br, bi):
    return ar * br - ai * bi, ar * bi + ai * br


def _store_twiddled(o_ref, lead, s, y, tw_ref):
    half = y.shape[0] // 2
    rep = y.shape[1] // LANES
    tr = pltpu.repeat(tw_ref[s, :half, :], rep, axis=1)
    ti = pltpu.repeat(tw_ref[s, half:, :], rep, axis=1)
    yr, yi = _cmul(y[:half], y[half:], tr, ti)
    o_ref[lead + (slice(0, half), s, slice(None))] = yr
    o_ref[lead + (slice(half, 2 * half), s, slice(None))] = yi


def _fft_stage1_kernel(f_ref, x_ref, tw_ref, o_ref, *, segs):
    for s in range(segs):
        y = _dot(f_ref[...], x_ref[:, s, :].astype(BF16))
        _store_twiddled(o_ref, (), s, y, tw_ref)


def _fft_stage1(x4, part, f1, tw_fwd, cb=512):
    _, ka, b, c = x4.shape
    rows = f1.shape[0]
    segs = SUBLANES
    cb = min(cb, c)
    return pl.pallas_call(
        functools.partial(_fft_stage1_kernel, segs=segs),
        grid=(b // segs, c // cb),
        in_specs=[
            pl.BlockSpec((rows, ka), lambda j, jc: (0, 0)),
            pl.BlockSpec((None, ka, segs, cb), lambda j, jc: (part, 0, j, jc)),
            pl.BlockSpec((segs, rows, LANES), lambda j, jc: (j, 0, 0)),
        ],
        out_specs=pl.BlockSpec((rows, segs, cb), lambda j, jc: (0, j, jc)),
        out_shape=_sds((rows, b, c), F32),
        compiler_params=_params("parallel", "parallel"),
        name="fft_stage1",
    )(f1[:, :ka], x4, tw_fwd)


def _fft_filter_stage1_kernel(zf_ref, zr_ref, w1_ref, b1_ref, w2_ref, b2_ref, w3_ref, b3_ref, fr_ref, w4_ref,
                              dl_ref, f_ref, tw_ref, o_ref, *, seq, segs):
    a_n = zf_ref.shape[1]
    inv = 1.0 / (seq - 1)

    def mlp(z):
        h = jnp.sin(fr_ref[0:1, :] * (_dot(z, w1_ref[...]) + b1_ref[...]))
        h = jnp.sin(fr_ref[1:2, :] * (_dot(h.astype(BF16), w2_ref[...]) + b2_ref[...]))
        h = jnp.sin(fr_ref[2:3, :] * (_dot(h.astype(BF16), w3_ref[...]) + b3_ref[...]))
        return h.astype(BF16)

    for s in range(segs):
        b = pl.program_id(0) * segs + s
        n = (lax.broadcasted_iota(jnp.int32, (a_n, 1), 0) * FFT_B + b).astype(F32)
        e_fwd = jnp.exp(-(n * inv) * dl_ref[...])
        e_rev = jnp.where(n == 0.0, 0.0, jnp.exp(-((seq - n) * inv) * dl_ref[...]))
        h_fwd = mlp(zf_ref[s])
        h_rev = mlp(zr_ref[s])
        for o in range(HY_ORDER):
            k = jnp.concatenate([_dot(h_fwd, w4_ref[2 * o]) * e_fwd, _dot(h_rev, w4_ref[2 * o + 1]) * e_rev], axis=0)
            _store_twiddled(o_ref, (o,), s, _dot(f_ref[...], k.astype(BF16)), tw_ref)


def _fft_filter_stage1(l, c, w1, b1, w2, b2, w3, b3, freq, w4, f1, tw_fwd, cb=256):
    fw = w1.shape[1]
    emb = 64
    a = l // FFT_B
    rows = f1.shape[0]
    segs = SUBLANES
    cb = min(cb, c)
    z = _hyena_pos_features(l)
    z_rev = jnp.concatenate([z[:1], z[:0:-1]], axis=0)
    by_b = lambda t: jnp.pad(t, ((0, 0), (0, emb - HY_EMB))).astype(BF16).reshape(a, FFT_B, emb).transpose(1, 0, 2)
    w1p = jnp.pad(w1, ((0, emb - HY_EMB), (0, 0))).astype(BF16)
    w4r = w4.astype(BF16).reshape(fw, HY_ORDER * 2, c).transpose(1, 0, 2)
    max_decay = math.log(HY_DECAY_TARGET) / HY_FAST_PCT
    min_decay = math.log(HY_DECAY_TARGET) / HY_SLOW_PCT
    deltas = jnp.abs(jnp.linspace(min_decay, max_decay, c, dtype=F32)).reshape(1, c)
    full = lambda shape: pl.BlockSpec(shape, lambda j, jc: (0,) * len(shape))
    feat = pl.BlockSpec((segs, a, emb), lambda j, jc: (j, 0, 0))
    return pl.pallas_call(
        functools.partial(_fft_filter_stage1_kernel, seq=l, segs=segs),
        grid=(FFT_B // segs, c // cb),
        in_specs=[
            feat, feat,
            full((emb, fw)), full((1, fw)), full((fw, fw)), full((1, fw)), full((fw, fw)), full((1, fw)),
            full((3, fw)),
            pl.BlockSpec((HY_ORDER * 2, fw, cb), lambda j, jc: (0, 0, jc)),
            pl.BlockSpec((1, cb), lambda j, jc: (0, jc)),
            full((rows, 2 * a)),
            pl.BlockSpec((segs, rows, LANES), lambda j, jc: (j, 0, 0)),
        ],
        out_specs=pl.BlockSpec((HY_ORDER, rows, segs, cb), lambda j, jc: (0, 0, j, jc)),
        out_shape=_sds((HY_ORDER, rows, FFT_B, c), F32),
        compiler_params=_params("parallel", "parallel"),
        name="fft_filter_stage1",
    )(by_b(z), by_b(z_rev), w1p, b1.reshape(1, fw), w2.astype(BF16), b2.reshape(1, fw), w3.astype(BF16),
      b3.reshape(1, fw), freq, w4r, deltas, f1, tw_fwd)


def _fft_spectrum_kernel(m_ref, y_ref, o_ref, *, kq):
    b = y_ref.shape[2]
    for q in range(kq):
        z = jnp.concatenate([y_ref[0, q], y_ref[1, q]], axis=0).astype(BF16)
        x = _dot(m_ref[...], z)
        o_ref[0, q] = x[:b]
        o_ref[1, q] = x[b:]


def _fft_spectrum(y5, order, m_fwd, kq=4):
    _, _, na, b, c = y5.shape
    kq = min(kq, na)
    return pl.pallas_call(
        functools.partial(_fft_spectrum_kernel, kq=kq),
        grid=(na // kq,),
        in_specs=[pl.BlockSpec((2 * b, 2 * b), lambda i: (0, 0)),
                  pl.BlockSpec((None, 2, kq, b, c), lambda i: (order, 0, i, 0, 0))],
        out_specs=pl.BlockSpec((2, kq, b, c), lambda i: (0, i, 0, 0)),
        out_shape=_sds((2, na, b, c), F32),
        compiler_params=_params("parallel"),
        name="fft_kernel_spectrum",
    )(m_fwd, y5)


def _fft_mid_kernel(mf_ref, mi_ref, y_ref, k_ref, tw_ref, o_ref, *, kq):
    b = y_ref.shape[2]
    c = y_ref.shape[3]
    for q in range(kq):
        z = jnp.concatenate([y_ref[0, q], y_ref[1, q]], axis=0).astype(BF16)
        x = _dot(mf_ref[...], z)
        pr, pi = _cmul(x[:b], x[b:], k_ref[0, q], k_ref[1, q])
        v = _dot(mi_ref[...], jnp.concatenate([pr, pi], axis=0).astype(BF16))
        tr = pltpu.repeat(tw_ref[q, :b, :], c // LANES, axis=1)
        ti = pltpu.repeat(tw_ref[q, b:, :], c // LANES, axis=1)
        vr, vi = _cmul(v[:b], v[b:], tr, ti)
        o_ref[0, q] = vr
        o_ref[1, q] = vi


def _fft_mid(y4, kspec, m_fwd, m_inv, tw_inv, kq=4):
    _, na, b, c = y4.shape
    kq = min(kq, na)
    blk = pl.BlockSpec((2, kq, b, c), lambda i: (0, i, 0, 0))
    mat = pl.BlockSpec((2 * b, 2 * b), lambda i: (0, 0))
    return pl.pallas_call(
        functools.partial(_fft_mid_kernel, kq=kq),
        grid=(na // kq,),
        in_specs=[mat, mat, blk, blk, pl.BlockSpec((kq, 2 * b, LANES), lambda i: (i, 0, 0))],
        out_specs=blk,
        out_shape=_sds(y4.shape, F32),
        compiler_params=_params("parallel"),
        name="fft_mid",
    )(m_fwd, m_inv, y4, kspec, tw_inv)


def _fft_stage3_kernel(f_ref, v_ref, u_ref, g_ref, sk_ref, ng_ref, o_ref, *, segs, head_dim):
    c = u_ref.shape[-1]
    for s in range(segs):
        y = _dot(f_ref[...], v_ref[:, s, :].astype(BF16))
        z = g_ref[:, s, :] * (y + u_ref[:, s, :] * sk_ref[...])
        if head_dim:
            heads = [z[:, h:h + head_dim] for h in range(0, c, head_dim)]
            heads = [zh * lax.rsqrt(jnp.mean(zh * zh, axis=-1, keepdims=True) + EPS) for zh in heads]
            z = jnp.concatenate(heads, axis=-1) * ng_ref[...]
        o_ref[:, s, :] = z


def _fft_stage3(v3, f3, u4, u_part, g4, g_part, skip, norm_g, head_dim, cb=512):
    a = f3.shape[0]
    rows, b, c = v3.shape
    segs = SUBLANES
    cb = min(cb, c)
    act = lambda part: pl.BlockSpec((None, a, segs, cb), lambda j, jc: (part, 0, j, jc))
    vec = pl.BlockSpec((1, cb), lambda j, jc: (0, jc))
    return pl.pallas_call(
        functools.partial(_fft_stage3_kernel, segs=segs, head_dim=head_dim),
        grid=(b // segs, c // cb),
        in_specs=[pl.BlockSpec((a, rows), lambda j, jc: (0, 0)),
                  pl.BlockSpec((rows, segs, cb), lambda j, jc: (0, j, jc)),
                  act(u_part), act(g_part), vec, vec],
        out_specs=pl.BlockSpec((a, segs, cb), lambda j, jc: (0, j, jc)),
        out_shape=_sds((a, b, c), F32),
        compiler_params=_params("parallel", "parallel"),
        name="fft_stage3",
    )(f3, v3, u4, g4, skip.reshape(1, c), norm_g.reshape(1, c))


def _hyena_mixer(proj, conv_w, conv_b, filt, skip, norm_g, c):
    l = proj.shape[0]
    a = l // FFT_B
    hyc4 = _dwconv1d(proj, conv_w, conv_b, 3, c).reshape(3, a, FFT_B, c)
    f1, m_fwd, m_inv, f3 = _dft_constants(l)
    tw_fwd, tw_inv = _twiddle_tables(l)
    yk = _fft_filter_stage1(l, c, *filt, f1, tw_fwd).reshape(HY_ORDER, 2, 2 * a, FFT_B, c)
    z4, part = hyc4, 0
    for o in range(HY_ORDER):
        kspec = _fft_spectrum(yk, o, m_fwd)
        y4 = _fft_stage1(z4, part, f1, tw_fwd).reshape(2, 2 * a, FFT_B, c)
        v3 = _fft_mid(y4, kspec, m_fwd, m_inv, tw_inv).reshape(4 * a, FFT_B, c)
        last = o == HY_ORDER - 1
        z = _fft_stage3(v3, f3, z4, part, hyc4, 1 + o, skip[o], norm_g, c // HY_HEADS if last else 0)
        z4, part = z.reshape(1, a, FFT_B, c), 0
    return z.reshape(l, c)


def _lower_bound(lg_ref, slot):
    lg = lg_ref[...]
    e = jnp.exp(lg - jnp.max(lg, axis=0, keepdims=True))
    sm = e / jnp.sum(e, axis=0, keepdims=True)
    return jnp.sum(sm[:slot + 1], axis=0, keepdims=True)


def _prefix_sum_rows(x):
    t = x.shape[0]
    row = lax.broadcasted_iota(jnp.int32, x.shape, 0)
    s = 1
    while s < t:
        x = x + jnp.where(row >= s, pltpu.roll(x, s, 0), 0.0)
        s *= 2
    return x


def _gla_kernel(*refs, slot, reverse, nchunk, chunk, with_out, combine):
    it = iter(refs)
    i_ref, f_ref, lg_ref, s0_ref = next(it), next(it), next(it), next(it)
    q_ref = next(it) if with_out else None
    ofw_ref, g_ref, ng_ref = (next(it), next(it), next(it)) if combine else (None, None, None)
    o_ref = next(it) if with_out else None
    sf_ref, st_ref = next(it), next(it)

    @pl.when(pl.program_id(1) == 0)
    def _():
        st_ref[...] = s0_ref[...]

    lb = _lower_bound(lg_ref, slot)
    row = lax.broadcasted_iota(jnp.int32, (chunk, chunk), 0)
    col = lax.broadcasted_iota(jnp.int32, (chunk, chunk), 1)
    keep = (col >= row) if reverse else (col <= row)
    nt = (((1,), (1,)), ((), ()))
    tn = (((0,), (0,)), ((), ()))
    st = st_ref[...]
    order = range(nchunk - 1, -1, -1) if reverse else range(nchunk)
    for ci in order:
        rows = slice(ci * chunk, (ci + 1) * chunk)
        f = lb + (1.0 - lb) * jax.nn.sigmoid(f_ref[rows, :])
        logf = jnp.log(f)
        key = 1.0 - f
        b = _prefix_sum_rows(logf)
        total = b[chunk - 1:chunk, :]
        if reverse:
            b = total - b + logf
        v = i_ref[rows, :].astype(BF16)
        k_state = (key * jnp.exp(total - b)).astype(BF16)
        if with_out:
            q = q_ref[rows, :]
            q_in = (q * jax.nn.sigmoid(q) * jnp.exp(b)).astype(BF16)
            k_intra = (key * jnp.exp(-b)).astype(BF16)
            att = lax.dot_general(q_in, k_intra, nt, preferred_element_type=F32)
            att = jnp.where(keep, att, 0.0).astype(BF16)
            o = _dot(att, v) + lax.dot_general(q_in, st.astype(BF16), nt, preferred_element_type=F32)
            if combine:
                o = o + ofw_ref[rows, :]
                o = o * lax.rsqrt(jnp.mean(o * o, axis=-1, keepdims=True) + EPS)
                g = g_ref[rows, :]
                o = o * ng_ref[...] * (g * jax.nn.sigmoid(g))
            o_ref[rows, :] = o.astype(o_ref.dtype)
        st = st * jnp.exp(total) + lax.dot_general(v, k_state, tn, preferred_element_type=F32)
    st_ref[...] = st

    @pl.when(pl.program_id(1) == pl.num_programs(1) - 1)
    def _():
        sf_ref[...] = st


def _gla(src, cols, lb_logits, slot, s0, reverse, with_out, fwd_out=None, norm_g=None, out_dtype=F32, tb=512):
    l = src.shape[0]
    heads = s0.shape[0]
    ke = HG_EXPAND
    tb = min(tb, l)
    nblk = l // tb
    combine = fwd_out is not None
    blk = (lambda j: nblk - 1 - j) if reverse else (lambda j: j)
    col = lambda name: pl.BlockSpec((tb, ke), lambda h, j: (blk(j), cols[name] + h))
    head_rows = pl.BlockSpec((tb, ke), lambda h, j: (blk(j), h))
    state = pl.BlockSpec((None, ke, ke), lambda h, j: (h, 0, 0))
    in_specs = [col("i"), col("f"), pl.BlockSpec((lb_logits.shape[0], ke), lambda h, j: (0, h)), state]
    args = [src, src, lb_logits, s0]
    out_specs, out_shape = [], []
    if with_out:
        in_specs.append(col("q"))
        args.append(src)
        out_specs.append(head_rows)
        out_shape.append(_sds((l, heads * ke), out_dtype))
    if combine:
        in_specs += [head_rows, col("g"), pl.BlockSpec((1, ke), lambda h, j: (0, h))]
        args += [fwd_out, src, norm_g.reshape(1, heads * ke)]
    out_specs.append(state)
    out_shape.append(_sds((heads, ke, ke), F32))
    res = pl.pallas_call(
        functools.partial(_gla_kernel, slot=slot, reverse=reverse, nchunk=tb // HG_CHUNK, chunk=HG_CHUNK,
                          with_out=with_out, combine=combine),
        grid=(heads, nblk),
        in_specs=in_specs,
        out_specs=out_specs,
        out_shape=out_shape,
        scratch_shapes=[pltpu.VMEM((ke, ke), F32)],
        compiler_params=_params("parallel", "arbitrary"),
        name="hgrn_gla",
    )(*args)
    return res if with_out else (None, res[0])


def _hgrn_mixer(proj, q0, ctx_proj, lb_logits, slot, norm_g, heads):
    qb = q0 // HG_EXPAND
    zeros = jnp.zeros((heads, HG_EXPAND, HG_EXPAND), F32)
    _, s0_fw = _gla(ctx_proj, {"i": 0, "f": heads}, lb_logits, slot, zeros, False, False)
    _, s0_bw = _gla(ctx_proj, {"i": 0, "f": 2 * heads}, lb_logits, slot, zeros, True, False)
    cols = {"q": qb, "i": qb + heads, "f": qb + 2 * heads, "g": qb + 4 * heads}
    o_fw, _ = _gla(proj, cols, lb_logits, slot, s0_fw, False, True)
    cols["f"] = qb + 3 * heads
    y, _ = _gla(proj, cols, lb_logits, slot, s0_bw, True, True, fwd_out=o_fw, norm_g=norm_g, out_dtype=BF16)
    return y


def _outproj_kernel(a_ref, b_ref, w_ref, x_ref, g_ref, o_ref):
    k1 = a_ref.shape[1]
    y = _dot(a_ref[...].astype(BF16), w_ref[:k1, :]) + _dot(b_ref[...].astype(BF16), w_ref[k1:, :])
    o_ref[...] = x_ref[...] + g_ref[...] * y


def _outproj(ya, yb, w, x, gate, tm=512):
    m, d = x.shape
    k1, k2 = ya.shape[1], yb.shape[1]
    tm = min(tm, m)
    return pl.pallas_call(
        _outproj_kernel,
        grid=(m // tm,),
        in_specs=[
            pl.BlockSpec((tm, k1), lambda i: (i, 0)),
            pl.BlockSpec((tm, k2), lambda i: (i, 0)),
            pl.BlockSpec((k1 + k2, d), lambda i: (0, 0)),
            pl.BlockSpec((tm, d), lambda i: (i, 0)),
            pl.BlockSpec((1, d), lambda i: (0, 0)),
        ],
        out_specs=pl.BlockSpec((tm, d), lambda i: (i, 0)),
        out_shape=_sds((m, d), F32),
        compiler_params=_params("parallel"),
        name="outproj_residual",
    )(ya, yb, w, x, gate)


def _ffn_kernel(hm_ref, hp_ref, hn_ref, wa_ref, wu_ref, cw_ref, cb_ref, wd_ref, x_ref, g_ref, fg_ref, o_ref,
                hext_ref, acc_ref, *, gw, final):
    i, j = pl.program_id(0), pl.program_id(1)
    tm = hm_ref.shape[0]
    n = tm + 2 * gw

    @pl.when(j == 0)
    def _():
        hext_ref[0:gw, :] = jnp.where(i > 0, hp_ref[...], jnp.zeros_like(hp_ref))
        hext_ref[gw:gw + tm, :] = hm_ref[...]
        hext_ref[gw + tm:n, :] = jnp.where(i < pl.num_programs(0) - 1, hn_ref[...], jnp.zeros_like(hn_ref))
        acc_ref[...] = jnp.zeros_like(acc_ref)

    a = _dot(hext_ref[...], wa_ref[...])
    u = _dot(hm_ref[...], wu_ref[...])
    col = lax.broadcasted_iota(jnp.int32, a.shape, 0) % gw
    a_m1 = jnp.where(col == 0, 0.0, pltpu.roll(a, 1, 0))
    a_p1 = jnp.where(col == gw - 1, 0.0, pltpu.roll(a, n - 1, 0))
    conv = cb_ref[...]
    for di in range(3):
        rows = slice(di * gw, di * gw + tm)
        conv = conv + (a_m1[rows] * cw_ref[3 * di:3 * di + 1, :] + a[rows] * cw_ref[3 * di + 1:3 * di + 2, :]
                       + a_p1[rows] * cw_ref[3 * di + 2:3 * di + 3, :])
    gelu = 0.5 * conv * (1.0 + lax.erf(conv * (1.0 / math.sqrt(2.0))))
    acc_ref[...] += _dot((gelu * u).astype(BF16), wd_ref[...])

    @pl.when(j == pl.num_programs(1) - 1)
    def _():
        y = x_ref[...] + g_ref[...] * acc_ref[...]
        if final:
            y = y * lax.rsqrt(jnp.mean(y * y, axis=-1, keepdims=True) + EPS) * fg_ref[...]
        o_ref[...] = y


def _conv_ffn(h, x, gate, w_up, conv_w, conv_b, w_down, final_g, gw, tm=512, tf=512):
    l, d = x.shape
    dff = w_down.shape[0]
    tm, tf = min(tm, l), min(tf, dff)
    r = tm // gw
    nj = dff // tf
    last = l // gw - 1
    final = final_g is not None
    fg = (final_g if final else jnp.ones((d,), F32)).reshape(1, d)
    return pl.pallas_call(
        functools.partial(_ffn_kernel, gw=gw, final=final),
        grid=(l // tm, nj),
        in_specs=[
            pl.BlockSpec((tm, d), lambda i, j: (i, 0)),
            pl.BlockSpec((gw, d), lambda i, j: (jnp.maximum(i * r - 1, 0), 0)),
            pl.BlockSpec((gw, d), lambda i, j: (jnp.minimum((i + 1) * r, last), 0)),
            pl.BlockSpec((d, tf), lambda i, j: (0, j)),
            pl.BlockSpec((d, tf), lambda i, j: (0, nj + j)),
            pl.BlockSpec((9, tf), lambda i, j: (0, j)),
            pl.BlockSpec((1, tf), lambda i, j: (0, j)),
            pl.BlockSpec((tf, d), lambda i, j: (j, 0)),
            pl.BlockSpec((tm, d), lambda i, j: (i, 0)),
            pl.BlockSpec((1, d), lambda i, j: (0, 0)),
            pl.BlockSpec((1, d), lambda i, j: (0, 0)),
        ],
        out_specs=pl.BlockSpec((tm, d), lambda i, j: (i, 0)),
        out_shape=_sds((l, d), F32),
        scratch_shapes=[pltpu.VMEM((tm + 2 * gw, d), BF16), pltpu.VMEM((tm, d), F32)],
        compiler_params=_params("parallel", "arbitrary"),
        name="conv_ffn",
    )(h, h, h, w_up, w_up, conv_w.reshape(9, dff), conv_b.reshape(1, dff), w_down, x, gate, fg)


def _pool_kernel(xm_ref, xp_ref, xn_ref, ng_ref, sc_ref, sh_ref, w_ref, b_ref, ps_ref, g_ref, o_ref, *, seq,
                 windows):
    i = pl.program_id(0)
    tm = xm_ref.shape[0]
    pd = w_ref.shape[1]
    norm = lambda x: _rms_mod(x, ng_ref[...], sc_ref[...], sh_ref[...])
    xm = xm_ref[...]
    hm = norm(xm)
    hp = jnp.where(i > 0, norm(xp_ref[...]), 0.0)
    hn = jnp.where(i < pl.num_programs(0) - 1, norm(xn_ref[...]), 0.0)
    t = i * tm + lax.broadcasted_iota(jnp.int32, (tm, 1), 0)
    n = tm + 2 * POOL_HALO
    for gi, win in enumerate(windows):
        cols = slice(gi * pd, (gi + 1) * pd)
        e = jnp.concatenate([hp[:, cols], hm[:, cols], hn[:, cols]], axis=0)
        p = e + pltpu.roll(e, 1, 0)
        w = 2
        while w < win:
            p = pltpu.roll(p, w // 2, 0) + pltpu.roll(p, n - w // 2, 0)
            w *= 2
        count = (jnp.clip(t + win // 2, 0, seq) - jnp.clip(t - win // 2, 0, seq)).astype(F32)
        mean = p[POOL_HALO:POOL_HALO + tm] / count
        y = _dot((mean - hm[:, cols]).astype(BF16), w_ref[gi]) + b_ref[gi]
        o_ref[:, cols] = xm[:, cols] + g_ref[:, cols] * (y * ps_ref[:, cols])


def _pool_layer(x, norm_g, sc, sh, w, b, scale, gate, tm=512):
    l, d = x.shape
    ng, pd, _ = w.shape
    assert max(POOL_WINDOWS) // 2 <= POOL_HALO == SUBLANES
    tm = min(tm, l)
    r = tm // SUBLANES
    last = l // SUBLANES - 1
    row = pl.BlockSpec((1, d), lambda i: (0, 0))
    return pl.pallas_call(
        functools.partial(_pool_kernel, seq=l, windows=POOL_WINDOWS),
        grid=(l // tm,),
        in_specs=[
            pl.BlockSpec((tm, d), lambda i: (i, 0)),
            pl.BlockSpec((SUBLANES, d), lambda i: (jnp.maximum(i * r - 1, 0), 0)),
            pl.BlockSpec((SUBLANES, d), lambda i: (jnp.minimum((i + 1) * r, last), 0)),
            row, row, row,
            pl.BlockSpec((ng, pd, pd), lambda i: (0, 0, 0)),
            pl.BlockSpec((ng, 1, pd), lambda i: (0, 0, 0)),
            row, row,
        ],
        out_specs=pl.BlockSpec((tm, d), lambda i: (i, 0)),
        out_shape=_sds((l, d), F32),
        compiler_params=_params("parallel"),
        name="pool_mixer_residual",
    )(x, x, x, norm_g.reshape(1, d), sc, sh, w.astype(BF16), b.reshape(ng, 1, pd), scale.reshape(1, d), gate)


def kernel(x, c, ctx, c_ctx, norm_mix_g, norm_ffn_g, mod_w, mod_b, in_w, in_b, hy_conv_w, hy_conv_b, hy_w1, hy_b1, hy_w2, hy_b2, hy_w3, hy_b3, hy_freq, hy_w4, hy_skip, hy_norm_g, hg_lb_logits, hg_norm_g, out_w, pool_w, pool_b, pool_scale, ffn_up_w, ffn_conv_w, ffn_conv_b, ffn_down_w, final_norm_g):
    batch, _, d = x.shape
    depth = mod_w.shape[0]
    hy_d = hy_norm_g.shape[1]
    hg_d = hg_norm_g.shape[1]
    hg_q0 = 3 * hy_d
    hg_i0 = hg_q0 + hg_d
    hg_heads = hg_d // HG_EXPAND
    outs = []
    for bi in range(batch):
        xs = x[bi]
        cvecs = jnp.zeros((SUBLANES, d), F32).at[0].set(c[bi]).at[1].set(c_ctx)
        for l in range(depth):
            mod = _adaln(cvecs, mod_w, mod_b, l)
            sh1, sc1, g1, sh2, sc2, g2 = [mod[0:1, k * d:(k + 1) * d] for k in range(N_MOD)]
            if l % 2 == 0:
                e = l // 2
                in_w_bf = in_w[e].astype(BF16)
                h = _norm_mod(xs, norm_mix_g[l], sc1, sh1)
                proj = _matmul_bias(h, in_w_bf, in_b[e])
                hc = _norm_mod(ctx[bi], norm_mix_g[l], mod[1:2, d:2 * d], mod[1:2, 0:d])
                ctx_proj = _matmul_bias(hc, in_w_bf[:, hg_i0:hg_i0 + 3 * hg_d], in_b[e, hg_i0:hg_i0 + 3 * hg_d])
                filt = (hy_w1[e], hy_b1[e], hy_w2[e], hy_b2[e], hy_w3[e], hy_b3[e], hy_freq[e], hy_w4[e])
                y_hy = _hyena_mixer(proj, hy_conv_w[e], hy_conv_b[e], filt, hy_skip[e], hy_norm_g[e], hy_d)
                y_hg = _hgrn_mixer(proj, hg_q0, ctx_proj, hg_lb_logits, e, hg_norm_g[e], hg_heads)
                xs = _outproj(y_hy, y_hg, out_w[e].astype(BF16), xs, g1)
            else:
                od = l // 2
                xs = _pool_layer(xs, norm_mix_g[l], sc1, sh1, pool_w[od], pool_b[od], pool_scale[od], g1)
            h = _norm_mod(xs, norm_ffn_g[l], sc2, sh2)
            xs = _conv_ffn(h, xs, g2, ffn_up_w[l].astype(BF16), ffn_conv_w[l], ffn_conv_b[l],
                           ffn_down_w[l].astype(BF16), final_norm_g if l == depth - 1 else None, GRID_W)
        outs.append(xs[None])
    return outs[0] if batch == 1 else jnp.concatenate(outs, axis=0)
```

```python
import functools
import math

import numpy as np
import jax
import jax.numpy as jnp
from jax import lax
from jax.experimental import pallas as pl
from jax.experimental.pallas import tpu as pltpu

F32 = jnp.float32
BF16 = jnp.bfloat16
U32 = jnp.uint32
EPS = 1e-6

LANES = 128
SUBLANES = 8
VMEM_LIMIT_BYTES = 56 * 1024 * 1024

GRID_W = 64
N_MOD = 6
HY_HEADS = 8
HY_ORDER = 2
HY_EMB = 33
HY_DECAY_TARGET = 1e-2
HY_FAST_PCT = 0.3
HY_SLOW_PCT = 1.5
HG_EXPAND = 128
HG_CHUNK = 64
POOL_WINDOWS = (2, 4, 8, 16)
POOL_HALO = 8
FFT_B = 128
FFN_ROWS = 1024


def _params(*sem):
    return pltpu.CompilerParams(dimension_semantics=sem, vmem_limit_bytes=VMEM_LIMIT_BYTES)


def _sds(shape, dtype):
    return jax.ShapeDtypeStruct(shape, dtype)


def _dot(a, b):
    return jnp.dot(a, b, preferred_element_type=F32)


def _adaln_kernel(c_ref, w_ref, b_ref, o_ref):
    c = c_ref[...]
    s = (c * jax.nn.sigmoid(c)).astype(BF16)
    o_ref[...] = _dot(s, w_ref[...].astype(BF16)) + b_ref[...]


def _adaln(cvecs, mod_w, mod_b, layer, tn=1024):
    _, d, n = mod_w.shape
    tn = min(tn, n)
    return pl.pallas_call(
        _adaln_kernel,
        grid=(n // tn,),
        in_specs=[
            pl.BlockSpec((SUBLANES, d), lambda j: (0, 0)),
            pl.BlockSpec((None, d, tn), lambda j: (layer, 0, j)),
            pl.BlockSpec((None, 1, tn), lambda j: (layer, 0, j)),
        ],
        out_specs=pl.BlockSpec((SUBLANES, tn), lambda j: (0, j)),
        out_shape=_sds((SUBLANES, n), F32),
        compiler_params=_params("arbitrary"),
        name="adaln",
    )(cvecs, mod_w, mod_b.reshape(mod_b.shape[0], 1, n))


def _rms_mod(x, g, sc, sh):
    y = x * lax.rsqrt(jnp.mean(x * x, axis=-1, keepdims=True) + EPS) * g
    return y * (1.0 + sc) + sh


def _norm_mod_kernel(x_ref, g_ref, sc_ref, sh_ref, o_ref):
    o_ref[...] = _rms_mod(x_ref[...], g_ref[...], sc_ref[...], sh_ref[...]).astype(o_ref.dtype)


def _norm_mod(x, g, sc, sh, tm=512):
    m, d = x.shape
    tm = min(tm, m)
    row = pl.BlockSpec((1, d), lambda i: (0, 0))
    return pl.pallas_call(
        _norm_mod_kernel,
        grid=(m // tm,),
        in_specs=[pl.BlockSpec((tm, d), lambda i: (i, 0)), row, row, row],
        out_specs=pl.BlockSpec((tm, d), lambda i: (i, 0)),
        out_shape=_sds((m, d), BF16),
        compiler_params=_params("parallel"),
        name="norm_mod",
    )(x, g.reshape(1, d), sc, sh)


def _matmul_bias_kernel(x_ref, w_ref, b_ref, o_ref):
    o_ref[...] = _dot(x_ref[...], w_ref[...]) + b_ref[...]


def _matmul_bias(x, w, b, tm=1024, tn=1024):
    m, k = x.shape
    n = w.shape[1]
    tm, tn = min(tm, m), min(tn, n)
    return pl.pallas_call(
        _matmul_bias_kernel,
        grid=(n // tn, m // tm),
        in_specs=[
            pl.BlockSpec((tm, k), lambda j, i: (i, 0)),
            pl.BlockSpec((k, tn), lambda j, i: (0, j)),
            pl.BlockSpec((1, tn), lambda j, i: (0, j)),
        ],
        out_specs=pl.BlockSpec((tm, tn), lambda j, i: (i, j)),
        out_shape=_sds((m, n), F32),
        compiler_params=_params("parallel", "parallel"),
        name="matmul_bias",
    )(x, w, b.reshape(1, n))


def _dwconv1d_kernel(xm_ref, xp_ref, xn_ref, w_ref, b_ref, o_ref):
    i = pl.program_id(0)
    x = xm_ref[...]
    tm = x.shape[0]
    prev_row = jnp.where(i > 0, xp_ref[SUBLANES - 1:SUBLANES, :], 0.0)
    next_row = jnp.where(i < pl.num_programs(0) - 1, xn_ref[0:1, :], 0.0)
    row = lax.broadcasted_iota(jnp.int32, x.shape, 0)
    x_m1 = jnp.where(row == 0, prev_row, pltpu.roll(x, 1, 0))
    x_p1 = jnp.where(row == tm - 1, next_row, pltpu.roll(x, tm - 1, 0))
    o_ref[...] = x_m1 * w_ref[0:1, :] + x * w_ref[1:2, :] + x_p1 * w_ref[2:3, :] + b_ref[...]


def _dwconv1d(proj, w, b, parts, c, tm=512):
    l = proj.shape[0]
    tm = min(tm, l)
    r = tm // SUBLANES
    last = l // SUBLANES - 1
    return pl.pallas_call(
        _dwconv1d_kernel,
        grid=(l // tm, parts),
        in_specs=[
            pl.BlockSpec((tm, c), lambda i, j: (i, j)),
            pl.BlockSpec((SUBLANES, c), lambda i, j: (jnp.maximum(i * r - 1, 0), j)),
            pl.BlockSpec((SUBLANES, c), lambda i, j: (jnp.minimum((i + 1) * r, last), j)),
            pl.BlockSpec((3, c), lambda i, j: (0, j)),
            pl.BlockSpec((1, c), lambda i, j: (0, j)),
        ],
        out_specs=pl.BlockSpec((None, tm, c), lambda i, j: (j, i, 0)),
        out_shape=_sds((parts, l, c), F32),
        compiler_params=_params("parallel", "parallel"),
        name="hyena_dwconv1d",
    )(proj, proj, proj, w, b.reshape(1, parts * c))


def _hyena_pos_features(l):
    t = jnp.linspace(0.0, 1.0, l, dtype=F32)[:, None]
    bands = (HY_EMB - 1) // 2
    f = jnp.linspace(1e-4, bands - 1, bands, dtype=F32)[None, :]
    w = (2.0 * math.pi / l) * jnp.arange(l, dtype=F32)[:, None]
    return jnp.concatenate([t, jnp.cos(f * w), -jnp.sin(f * w)], axis=-1)


def _dft_constants(l):
    b = FFT_B
    a = l // b
    na = 2 * a
    n = 2 * l
    ang1 = 2.0 * np.pi * np.outer(np.arange(na), np.arange(na)) / na
    f1 = np.concatenate([np.cos(ang1), -np.sin(ang1)], axis=0)
    angb = 2.0 * np.pi * np.outer(np.arange(b), np.arange(b)) / b
    cb, sb = np.cos(angb), np.sin(angb)
    m_fwd = np.block([[cb, sb], [-sb, cb]])
    m_inv = np.block([[cb, -sb], [sb, cb]])
    f3 = np.concatenate([np.cos(ang1[:a]), -np.sin(ang1[:a])], axis=1) / n
    as_bf16 = lambda m: jnp.asarray(m, dtype=F32).astype(BF16)
    return as_bf16(f1), as_bf16(m_fwd), as_bf16(m_inv), as_bf16(f3)


def _twiddle_tables(l):
    b = FFT_B
    na = 2 * (l // b)
    n = 2 * l
    kap = jnp.arange(na, dtype=jnp.int32)
    bi = jnp.arange(b, dtype=jnp.int32)
    ang = ((bi[:, None] * kap[None, :]) % n).astype(F32) * (2.0 * math.pi / n)
    cos, sin = jnp.cos(ang), jnp.sin(ang)
    rep = lambda t: jnp.broadcast_to(t[:, :, None], t.shape + (LANES,))
    tw_fwd = rep(jnp.concatenate([cos, -sin], axis=1))
    tw_inv = rep(jnp.concatenate([cos.T, sin.T], axis=1))
    return tw_fwd, tw_inv


def _cmul(ar, ai, br, bi):
    return ar * br - ai * bi, ar * bi + ai * br


def _pack_complex(re, im):
    r = lax.bitcast_convert_type(re.astype(BF16).astype(F32), U32)
    i = lax.bitcast_convert_type(im.astype(BF16).astype(F32), U32)
    return r | (i >> 16)


def _unpack_complex(w):
    re = lax.bitcast_convert_type(w & jnp.uint32(0xFFFF0000), F32)
    im = lax.bitcast_convert_type(w << 16, F32)
    return re, im


def _stacked_bf16(w):
    re, im = _unpack_complex(w)
    return jnp.concatenate([re, im], axis=0).astype(BF16)


def _strided_rows(chunk_refs, s, rows, stride):
    flat = [r.reshape(rows * stride, LANES) for r in chunk_refs]
    return jnp.concatenate([r[pl.ds(s, rows, stride=stride), :] for r in flat], axis=-1)


def _twiddled_packed(y, tw_ref, s):
    half = y.shape[0] // 2
    rep = y.shape[1] // LANES
    tr = jnp.concatenate([tw_ref[s, :half, :]] * rep, axis=1)
    ti = jnp.concatenate([tw_ref[s, half:, :]] * rep, axis=1)
    return _pack_complex(*_cmul(y[:half], y[half:], tr, ti))


def _fft_stage1_kernel(*refs, segs, nq, strided):
    f_ref, x_refs, tw_ref, o_ref = refs[0], refs[1:1 + nq], refs[1 + nq], refs[2 + nq]
    ka = f_ref.shape[1]
    for s in range(segs):
        x = _strided_rows(x_refs, s, ka, segs) if strided else x_refs[0][s]
        o_ref[s] = _twiddled_packed(_dot(f_ref[...], x.astype(BF16)), tw_ref, s)


def _fft_stage1(x4, part, f1, tw_fwd, strided, cb=512):
    if strided:
        _, ka, b, c = x4.shape
    else:
        _, b, ka, c = x4.shape
    rows = f1.shape[0]
    segs = SUBLANES
    cb = min(cb, c)
    if strided:
        nq = cb // LANES
        x_specs = [pl.BlockSpec((None, ka, segs, LANES), functools.partial(lambda j, jc, q: (part, 0, j, jc * nq + q), q=q))
                   for q in range(nq)]
    else:
        nq = 1
        x_specs = [pl.BlockSpec((None, segs, ka, cb), lambda j, jc: (part, j, 0, jc))]
    return pl.pallas_call(
        functools.partial(_fft_stage1_kernel, segs=segs, nq=nq, strided=strided),
        grid=(b // segs, c // cb),
        in_specs=[pl.BlockSpec((rows, ka), lambda j, jc: (0, 0))] + x_specs
        + [pl.BlockSpec((segs, rows, LANES), lambda j, jc: (j, 0, 0))],
        out_specs=pl.BlockSpec((segs, rows // 2, cb), lambda j, jc: (j, 0, jc)),
        out_shape=_sds((b, rows // 2, c), U32),
        compiler_params=_params("parallel", "parallel"),
        name="fft_stage1",
    )(f1[:, :ka], *([x4] * nq), tw_fwd)


def _fft_filter_stage1_kernel(z_ref, w1_ref, b1_ref, w2_ref, b2_ref, w3_ref, b3_ref, fr_ref, w4_ref, dl_ref, f_ref,
                              tw_ref, o_ref, h_ref, *, seq, segs):
    a_n = z_ref.shape[1]
    inv = 1.0 / (seq - 1)

    @pl.when(pl.program_id(1) == 0)
    def _():
        for s in range(segs):
            h = jnp.sin(fr_ref[0:1, :] * (_dot(z_ref[s], w1_ref[...]) + b1_ref[...]))
            h = jnp.sin(fr_ref[1:2, :] * (_dot(h.astype(BF16), w2_ref[...]) + b2_ref[...]))
            h = jnp.sin(fr_ref[2:3, :] * (_dot(h.astype(BF16), w3_ref[...]) + b3_ref[...]))
            h_ref[s] = h.astype(BF16)

    for s in range(segs):
        b = pl.program_id(0) * segs + s
        n = (lax.broadcasted_iota(jnp.int32, (a_n, 1), 0) * FFT_B + b).astype(F32)
        e_fwd = jnp.exp(-(n * inv) * dl_ref[...])
        e_rev = jnp.where(n == 0.0, 0.0, jnp.exp(-((seq - n) * inv) * dl_ref[...]))
        h = h_ref[s]
        for o in range(HY_ORDER):
            k = jnp.concatenate([_dot(h, w4_ref[2 * o]) * e_fwd, _dot(h, w4_ref[2 * o + 1]) * e_rev], axis=0)
            o_ref[o, s] = _twiddled_packed(_dot(f_ref[...], k.astype(BF16)), tw_ref, s)


def _block_diag2(w):
    z = jnp.zeros_like(w)
    return jnp.concatenate([jnp.concatenate([w, z], axis=1), jnp.concatenate([z, w], axis=1)], axis=0)


def _fft_filter_stage1(l, c, w1, b1, w2, b2, w3, b3, freq, w4, f1, tw_fwd, cb=256):
    fw = w1.shape[1]
    emb = 64
    a = l // FFT_B
    rows = f1.shape[0]
    segs = SUBLANES
    cb = min(cb, c)
    z = _hyena_pos_features(l)
    z_rev = jnp.concatenate([z[:1], z[:0:-1]], axis=0)
    pad = lambda t: jnp.pad(t, ((0, 0), (0, emb - HY_EMB)))
    z2 = jnp.concatenate([pad(z), pad(z_rev)], axis=1).astype(BF16)
    z2 = z2.reshape(a, FFT_B, 2 * emb).transpose(1, 0, 2)
    both = lambda v: jnp.concatenate([v, v]).reshape(1, 2 * fw)
    w1d = _block_diag2(jnp.pad(w1, ((0, emb - HY_EMB), (0, 0)))).astype(BF16)
    w4r = w4.reshape(fw, HY_ORDER * 2, c).transpose(1, 0, 2)
    zeros = jnp.zeros_like(w4r)
    fwd_dir = (jnp.arange(HY_ORDER * 2) % 2 == 0)[:, None, None]
    w4d = jnp.concatenate([jnp.where(fwd_dir, w4r, zeros), jnp.where(fwd_dir, zeros, w4r)], axis=1).astype(BF16)
    max_decay = math.log(HY_DECAY_TARGET) / HY_FAST_PCT
    min_decay = math.log(HY_DECAY_TARGET) / HY_SLOW_PCT
    deltas = jnp.abs(jnp.linspace(min_decay, max_decay, c, dtype=F32)).reshape(1, c)
    full = lambda shape: pl.BlockSpec(shape, lambda j, jc: (0,) * len(shape))
    return pl.pallas_call(
        functools.partial(_fft_filter_stage1_kernel, seq=l, segs=segs),
        grid=(FFT_B // segs, c // cb),
        in_specs=[
            pl.BlockSpec((segs, a, 2 * emb), lambda j, jc: (j, 0, 0)),
            full((2 * emb, 2 * fw)), full((1, 2 * fw)), full((2 * fw, 2 * fw)), full((1, 2 * fw)),
            full((2 * fw, 2 * fw)), full((1, 2 * fw)), full((3, 2 * fw)),
            pl.BlockSpec((HY_ORDER * 2, 2 * fw, cb), lambda j, jc: (0, 0, jc)),
            pl.BlockSpec((1, cb), lambda j, jc: (0, jc)),
            full((rows, 2 * a)),
            pl.BlockSpec((segs, rows, LANES), lambda j, jc: (j, 0, 0)),
        ],
        out_specs=pl.BlockSpec((HY_ORDER, segs, rows // 2, cb), lambda j, jc: (0, j, 0, jc)),
        out_shape=_sds((HY_ORDER, FFT_B, rows // 2, c), U32),
        scratch_shapes=[pltpu.VMEM((segs, a, 2 * fw), BF16)],
        compiler_params=_params("parallel", "arbitrary"),
        name="fft_filter_stage1",
    )(z2, w1d, both(b1), _block_diag2(w2).astype(BF16), both(b2), _block_diag2(w3).astype(BF16), both(b3),
      jnp.concatenate([freq, freq], axis=1), w4d, deltas, f1, tw_fwd)


def _fft_mid_kernel(*refs, kq, nc):
    mf_ref, mi_ref = refs[0], refs[1]
    y_refs, k_refs = refs[2:2 + nc], refs[2 + nc:2 + 2 * nc]
    tw_ref, o_ref = refs[2 + 2 * nc], refs[3 + 2 * nc]
    b = mf_ref.shape[0] // 2
    for q in range(kq):
        x = _dot(mf_ref[...], _stacked_bf16(_strided_rows(y_refs, q, b, kq)))
        k = _dot(mf_ref[...], _stacked_bf16(_strided_rows(k_refs, q, b, kq)))
        pr, pi = _cmul(x[:b], x[b:], k[:b], k[b:])
        v = _dot(mi_ref[...], jnp.concatenate([pr, pi], axis=0).astype(BF16))
        o_ref[q] = _twiddled_packed(v, tw_ref, q)


def _fft_mid(y3, yk4, order, m_fwd, m_inv, tw_inv):
    b, na, c = y3.shape
    kq = SUBLANES
    nc = c // LANES
    mat = pl.BlockSpec((2 * b, 2 * b), lambda i: (0, 0))
    y_specs = [pl.BlockSpec((b, kq, LANES), functools.partial(lambda i, q: (0, i, q), q=q)) for q in range(nc)]
    k_specs = [pl.BlockSpec((None, b, kq, LANES), functools.partial(lambda i, q: (order, 0, i, q), q=q))
               for q in range(nc)]
    return pl.pallas_call(
        functools.partial(_fft_mid_kernel, kq=kq, nc=nc),
        grid=(na // kq,),
        in_specs=[mat, mat] + y_specs + k_specs + [pl.BlockSpec((kq, 2 * b, LANES), lambda i: (i, 0, 0))],
        out_specs=pl.BlockSpec((kq, b, c), lambda i: (i, 0, 0)),
        out_shape=_sds((na, b, c), U32),
        compiler_params=_params("parallel"),
        name="fft_mid",
    )(m_fwd, m_inv, *([y3] * nc), *([yk4] * nc), tw_inv)


def _fft_stage3_kernel(*refs, segs, nq, u_strided, head_dim):
    f_ref, v_refs = refs[0], refs[1:1 + nq]
    nu = nq if u_strided else 1
    u_refs, g_refs = refs[1 + nq:1 + nq + nu], refs[1 + nq + nu:1 + 2 * nq + nu]
    sk_ref, ng_ref, o_ref = refs[1 + 2 * nq + nu:]
    a = f_ref.shape[0]
    rows = f_ref.shape[1] // 2
    c = o_ref.shape[-1]
    for s in range(segs):
        y = _dot(f_ref[...], _stacked_bf16(_strided_rows(v_refs, s, rows, segs)))
        u = _strided_rows(u_refs, s, a, segs) if u_strided else u_refs[0][s]
        z = _strided_rows(g_refs, s, a, segs) * (y + u * sk_ref[...])
        if head_dim:
            heads = [z[:, h:h + head_dim] for h in range(0, c, head_dim)]
            heads = [zh * lax.rsqrt(jnp.mean(zh * zh, axis=-1, keepdims=True) + EPS) for zh in heads]
            z = jnp.concatenate(heads, axis=-1) * ng_ref[...]
        o_ref[s] = z


def _fft_stage3(v3, f3, u4, u_part, u_strided, g4, g_part, skip, norm_g, head_dim, cb=512):
    a = f3.shape[0]
    rows, b, c = v3.shape
    segs = SUBLANES
    cb = min(cb, c)
    nq = cb // LANES
    chunk = lambda shape, idx: [pl.BlockSpec(shape, functools.partial(idx, q=q)) for q in range(nq)]
    v_specs = chunk((rows, segs, LANES), lambda j, jc, q: (0, j, jc * nq + q))
    nat = lambda part: chunk((None, a, segs, LANES), lambda j, jc, q: (part, 0, j, jc * nq + q))
    u_specs = nat(u_part) if u_strided else [pl.BlockSpec((None, segs, a, cb), lambda j, jc: (u_part, j, 0, jc))]
    vec = pl.BlockSpec((1, cb), lambda j, jc: (0, jc))
    return pl.pallas_call(
        functools.partial(_fft_stage3_kernel, segs=segs, nq=nq, u_strided=u_strided, head_dim=head_dim),
        grid=(b // segs, c // cb),
        in_specs=[pl.BlockSpec((a, 2 * rows), lambda j, jc: (0, 0))] + v_specs + u_specs + nat(g_part) + [vec, vec],
        out_specs=pl.BlockSpec((segs, a, cb), lambda j, jc: (j, 0, jc)),
        out_shape=_sds((b, a, c), F32),
        compiler_params=_params("parallel", "parallel"),
        name="fft_stage3",
    )(f3, *([v3] * nq), *([u4] * (nq if u_strided else 1)), *([g4] * nq), skip.reshape(1, c), norm_g.reshape(1, c))


def _hyena_mixer(proj, conv_w, conv_b, filt, skip, norm_g, c):
    l = proj.shape[0]
    a = l // FFT_B
    hyc4 = _dwconv1d(proj, conv_w, conv_b, 3, c).reshape(3, a, FFT_B, c)
    f1, m_fwd, m_inv, f3 = _dft_constants(l)
    tw_fwd, tw_inv = _twiddle_tables(l)
    yk = _fft_filter_stage1(l, c, *filt, f1, tw_fwd)
    z4, part, strided = hyc4, 0, True
    for o in range(HY_ORDER):
        y3 = _fft_stage1(z4, part, f1, tw_fwd, strided)
        v3 = _fft_mid(y3, yk, o, m_fwd, m_inv, tw_inv)
        last = o == HY_ORDER - 1
        z = _fft_stage3(v3, f3, z4, part, strided, hyc4, 1 + o, skip[o], norm_g, c // HY_HEADS if last else 0)
        z4, part, strided = z[None], 0, False
    return z


def _lower_bound(lg_ref, slot):
    lg = lg_ref[...]
    e = jnp.exp(lg - jnp.max(lg, axis=0, keepdims=True))
    sm = e / jnp.sum(e, axis=0, keepdims=True)
    return jnp.sum(sm[:slot + 1], axis=0, keepdims=True)


def _prefix_sum_rows(x):
    t = x.shape[0]
    row = lax.broadcasted_iota(jnp.int32, x.shape, 0)
    s = 1
    while s < t:
        x = x + jnp.where(row >= s, pltpu.roll(x, s, 0), 0.0)
        s *= 2
    return x


def _gla_kernel(*refs, slot, reverse, nchunk, chunk, hpb, with_out, combine):
    it = iter(refs)
    i_ref, f_ref, lg_ref, s0_ref = next(it), next(it), next(it), next(it)
    q_ref = next(it) if with_out else None
    ofw_ref, g_ref, ng_ref = (next(it), next(it), next(it)) if combine else (None, None, None)
    o_ref = next(it) if with_out else None
    sf_ref, st_ref = next(it), next(it)
    ke = HG_EXPAND

    @pl.when(pl.program_id(1) == 0)
    def _():
        st_ref[...] = s0_ref[...]

    lb = _lower_bound(lg_ref, slot)
    row = lax.broadcasted_iota(jnp.int32, (chunk, chunk), 0)
    col = lax.broadcasted_iota(jnp.int32, (chunk, chunk), 1)
    keep = (col >= row) if reverse else (col <= row)
    nt = (((1,), (1,)), ((), ()))
    tn = (((0,), (0,)), ((), ()))
    st = [st_ref[h] for h in range(hpb)]
    order = range(nchunk - 1, -1, -1) if reverse else range(nchunk)
    for ci in order:
        rows = slice(ci * chunk, (ci + 1) * chunk)
        f = lb + (1.0 - lb) * jax.nn.sigmoid(f_ref[rows, :])
        logf = jnp.log(f)
        key = 1.0 - f
        b = _prefix_sum_rows(logf)
        total = b[chunk - 1:chunk, :]
        if reverse:
            b = total - b + logf
        v = i_ref[rows, :].astype(BF16)
        k_state = (key * jnp.exp(total - b)).astype(BF16)
        decay = jnp.exp(total)
        if with_out:
            q = q_ref[rows, :]
            q_in = (q * jax.nn.sigmoid(q) * jnp.exp(b)).astype(BF16)
            k_intra = (key * jnp.exp(-b)).astype(BF16)
        outs = []
        for h in range(hpb):
            hl = slice(h * ke, (h + 1) * ke)
            if with_out:
                att = lax.dot_general(q_in[:, hl], k_intra[:, hl], nt, preferred_element_type=F32)
                att = jnp.where(keep, att, 0.0).astype(BF16)
                o = _dot(att, v[:, hl]) + lax.dot_general(q_in[:, hl], st[h].astype(BF16), nt,
                                                           preferred_element_type=F32)
                if combine:
                    o = o + ofw_ref[rows, hl]
                    o = o * lax.rsqrt(jnp.mean(o * o, axis=-1, keepdims=True) + EPS)
                outs.append(o)
            st[h] = st[h] * decay[:, hl] + lax.dot_general(v[:, hl], k_state[:, hl], tn, preferred_element_type=F32)
        if with_out:
            o = jnp.concatenate(outs, axis=-1)
            if combine:
                g = g_ref[rows, :]
                o = o * ng_ref[...] * (g * jax.nn.sigmoid(g))
            o_ref[rows, :] = o.astype(o_ref.dtype)
    for h in range(hpb):
        st_ref[h] = st[h]

    @pl.when(pl.program_id(1) == pl.num_programs(1) - 1)
    def _():
        for h in range(hpb):
            sf_ref[h] = st[h]


def _gla(src, cols, lb_logits, slot, s0, reverse, with_out, fwd_out=None, norm_g=None, out_dtype=F32, tb=512, hpb=2):
    l = src.shape[0]
    heads = s0.shape[0]
    ke = HG_EXPAND
    tb = min(tb, l)
    nblk = l // tb
    if heads % hpb or any(off % hpb for off in cols.values()):
        hpb = 1
    wd = hpb * ke
    combine = fwd_out is not None
    blk = (lambda j: nblk - 1 - j) if reverse else (lambda j: j)
    col = lambda name: pl.BlockSpec((tb, wd), lambda h, j: (blk(j), cols[name] // hpb + h))
    head_rows = pl.BlockSpec((tb, wd), lambda h, j: (blk(j), h))
    state = pl.BlockSpec((hpb, ke, ke), lambda h, j: (h, 0, 0))
    in_specs = [col("i"), col("f"), pl.BlockSpec((lb_logits.shape[0], wd), lambda h, j: (0, h)), state]
    args = [src, src, lb_logits, s0]
    out_specs, out_shape = [], []
    if with_out:
        in_specs.append(col("q"))
        args.append(src)
        out_specs.append(head_rows)
        out_shape.append(_sds((l, heads * ke), out_dtype))
    if combine:
        in_specs += [head_rows, col("g"), pl.BlockSpec((1, wd), lambda h, j: (0, h))]
        args += [fwd_out, src, norm_g.reshape(1, heads * ke)]
    out_specs.append(state)
    out_shape.append(_sds((heads, ke, ke), F32))
    res = pl.pallas_call(
        functools.partial(_gla_kernel, slot=slot, reverse=reverse, nchunk=tb // HG_CHUNK, chunk=HG_CHUNK, hpb=hpb,
                          with_out=with_out, combine=combine),
        grid=(heads // hpb, nblk),
        in_specs=in_specs,
        out_specs=out_specs,
        out_shape=out_shape,
        scratch_shapes=[pltpu.VMEM((hpb, ke, ke), F32)],
        compiler_params=_params("parallel", "arbitrary"),
        name="hgrn_gla",
    )(*args)
    return res if with_out else (None, res[0])


def _hgrn_mixer(proj, q0, ctx_proj, lb_logits, slot, norm_g, heads):
    qb = q0 // HG_EXPAND
    zeros = jnp.zeros((heads, HG_EXPAND, HG_EXPAND), F32)
    _, s0_fw = _gla(ctx_proj, {"i": 0, "f": heads}, lb_logits, slot, zeros, False, False)
    _, s0_bw = _gla(ctx_proj, {"i": 0, "f": 2 * heads}, lb_logits, slot, zeros, True, False)
    cols = {"q": qb, "i": qb + heads, "f": qb + 2 * heads, "g": qb + 4 * heads}
    o_fw, _ = _gla(proj, cols, lb_logits, slot, s0_fw, False, True)
    cols["f"] = qb + 3 * heads
    y, _ = _gla(proj, cols, lb_logits, slot, s0_bw, True, True, fwd_out=o_fw, norm_g=norm_g, out_dtype=BF16)
    return y


def _outproj_kernel(*refs, na, nc):
    a_refs = refs[:nc]
    b_ref, w_ref, x_ref, g_ref, o_ref = refs[nc:]
    rows = a_refs[0].shape[0]
    k1 = nc * LANES
    ya = jnp.concatenate([_strided_rows(a_refs, s, rows, na) for s in range(na)], axis=0).astype(BF16)
    y = _dot(ya, w_ref[:k1, :]) + _dot(b_ref[...], w_ref[k1:, :])
    o_ref[...] = x_ref[...] + g_ref[...] * y


def _outproj(ya3, yb, w, x, gate, tn=1024):
    m, d = x.shape
    b, a, k1 = ya3.shape
    k2 = yb.shape[1]
    na = min(SUBLANES, a)
    tm = na * b
    tn = min(tn, d)
    nc = k1 // LANES
    a_specs = [pl.BlockSpec((b, na, LANES), functools.partial(lambda j, i, q: (0, i, q), q=q)) for q in range(nc)]
    return pl.pallas_call(
        functools.partial(_outproj_kernel, na=na, nc=nc),
        grid=(d // tn, m // tm),
        in_specs=a_specs + [
            pl.BlockSpec((tm, k2), lambda j, i: (i, 0)),
            pl.BlockSpec((k1 + k2, tn), lambda j, i: (0, j)),
            pl.BlockSpec((tm, tn), lambda j, i: (i, j)),
            pl.BlockSpec((1, tn), lambda j, i: (0, j)),
        ],
        out_specs=pl.BlockSpec((tm, tn), lambda j, i: (i, j)),
        out_shape=_sds((m, d), F32),
        compiler_params=_params("parallel", "parallel"),
        name="outproj_residual",
    )(*([ya3] * nc), yb, w, x, gate)


def _ffn_kernel(h_hbm, wa_ref, wu_ref, cw_ref, cb_ref, wd_ref, x_hbm, g_ref, fg_ref, o_ref,
                hext_ref, x_ref, sem, *, gw, final):
    i, j = pl.program_id(0), pl.program_id(1)
    ni, nj = pl.num_programs(0), pl.num_programs(1)
    tm = o_ref.shape[0]
    n = tm + 2 * gw
    row0 = i * tm

    def h_copy(src_row, dst_row, rows, k):
        return pltpu.make_async_copy(h_hbm.at[pl.ds(src_row, rows)], hext_ref.at[pl.ds(dst_row, rows)], sem.at[k])

    x_copy = pltpu.make_async_copy(x_hbm.at[pl.ds(row0, tm)], x_ref, sem.at[3])
    has_prev, has_next = i > 0, i < ni - 1

    @pl.when(j == 0)
    def _():
        h_copy(row0, gw, tm, 0).start()
        x_copy.start()

        @pl.when(has_prev)
        def _():
            h_copy(row0 - gw, 0, gw, 1).start()

        @pl.when(has_next)
        def _():
            h_copy(row0 + tm, gw + tm, gw, 2).start()

        @pl.when(jnp.logical_not(has_prev))
        def _():
            hext_ref[0:gw, :] = jnp.zeros((gw, hext_ref.shape[1]), hext_ref.dtype)

        @pl.when(jnp.logical_not(has_next))
        def _():
            hext_ref[gw + tm:n, :] = jnp.zeros((gw, hext_ref.shape[1]), hext_ref.dtype)

        o_ref[...] = jnp.zeros_like(o_ref)
        h_copy(row0, gw, tm, 0).wait()

        @pl.when(has_prev)
        def _():
            h_copy(row0 - gw, 0, gw, 1).wait()

        @pl.when(has_next)
        def _():
            h_copy(row0 + tm, gw + tm, gw, 2).wait()

    a = _dot(hext_ref[...], wa_ref[...])
    u = _dot(hext_ref[gw:gw + tm, :], wu_ref[...])
    col = lax.broadcasted_iota(jnp.int32, a.shape, 0) % gw
    a_m1 = jnp.where(col == 0, 0.0, pltpu.roll(a, 1, 0))
    a_p1 = jnp.where(col == gw - 1, 0.0, pltpu.roll(a, n - 1, 0))
    conv = cb_ref[...]
    for di in range(3):
        rows = slice(di * gw, di * gw + tm)
        conv = conv + (a_m1[rows] * cw_ref[3 * di:3 * di + 1, :] + a[rows] * cw_ref[3 * di + 1:3 * di + 2, :]
                       + a_p1[rows] * cw_ref[3 * di + 2:3 * di + 3, :])
    gelu = 0.5 * conv * (1.0 + lax.erf(conv * (1.0 / math.sqrt(2.0))))
    o_ref[...] += _dot((gelu * u).astype(BF16), wd_ref[...])

    @pl.when(j == nj - 1)
    def _():
        x_copy.wait()
        y = x_ref[...] + g_ref[...] * o_ref[...]
        if final:
            y = y * lax.rsqrt(jnp.mean(y * y, axis=-1, keepdims=True) + EPS) * fg_ref[...]
        o_ref[...] = y


def _conv_ffn(h, x, gate, w_up, conv_w, conv_b, w_down, final_g, gw, tf=512):
    l, d = x.shape
    dff = w_down.shape[0]
    tm, tf = min(FFN_ROWS, l), min(tf, dff)
    nj = dff // tf
    final = final_g is not None
    fg = (final_g if final else jnp.ones((d,), F32)).reshape(1, d)
    hbm = pl.BlockSpec(memory_space=pl.ANY)
    return pl.pallas_call(
        functools.partial(_ffn_kernel, gw=gw, final=final),
        grid=(l // tm, nj),
        in_specs=[
            hbm,
            pl.BlockSpec((d, tf), lambda i, j: (0, j)),
            pl.BlockSpec((d, tf), lambda i, j: (0, nj + j)),
            pl.BlockSpec((9, tf), lambda i, j: (0, j)),
            pl.BlockSpec((1, tf), lambda i, j: (0, j)),
            pl.BlockSpec((tf, d), lambda i, j: (j, 0)),
            hbm,
            pl.BlockSpec((1, d), lambda i, j: (0, 0)),
            pl.BlockSpec((1, d), lambda i, j: (0, 0)),
        ],
        out_specs=pl.BlockSpec((tm, d), lambda i, j: (i, 0)),
        out_shape=_sds((l, d), F32),
        scratch_shapes=[pltpu.VMEM((tm + 2 * gw, d), BF16), pltpu.VMEM((tm, d), F32),
                        pltpu.SemaphoreType.DMA((4,))],
        compiler_params=_params("arbitrary", "arbitrary"),
        name="conv_ffn",
    )(h, w_up, w_up, conv_w.reshape(9, dff), conv_b.reshape(1, dff), w_down, x, gate, fg)


def _pool_kernel(xm_ref, xp_ref, xn_ref, ng_ref, sc_ref, sh_ref, w_ref, b_ref, ps_ref, g_ref, o_ref, *, seq,
                 windows):
    i = pl.program_id(0)
    tm = xm_ref.shape[0]
    pd = w_ref.shape[1]
    norm = lambda x: _rms_mod(x, ng_ref[...], sc_ref[...], sh_ref[...])
    xm = xm_ref[...]
    hm = norm(xm)
    hp = jnp.where(i > 0, norm(xp_ref[...]), 0.0)
    hn = jnp.where(i < pl.num_programs(0) - 1, norm(xn_ref[...]), 0.0)
    t = i * tm + lax.broadcasted_iota(jnp.int32, (tm, 1), 0)
    n = tm + 2 * POOL_HALO
    for gi, win in enumerate(windows):
        cols = slice(gi * pd, (gi + 1) * pd)
        e = jnp.concatenate([hp[:, cols], hm[:, cols], hn[:, cols]], axis=0)
        p = e + pltpu.roll(e, 1, 0)
        w = 2
        while w < win:
            p = pltpu.roll(p, w // 2, 0) + pltpu.roll(p, n - w // 2, 0)
            w *= 2
        count = (jnp.clip(t + win // 2, 0, seq) - jnp.clip(t - win // 2, 0, seq)).astype(F32)
        mean = p[POOL_HALO:POOL_HALO + tm] / count
        y = _dot((mean - hm[:, cols]).astype(BF16), w_ref[gi]) + b_ref[gi]
        o_ref[:, cols] = xm[:, cols] + g_ref[:, cols] * (y * ps_ref[:, cols])


def _pool_layer(x, norm_g, sc, sh, w, b, scale, gate, tm=512):
    l, d = x.shape
    ng, pd, _ = w.shape
    assert max(POOL_WINDOWS) // 2 <= POOL_HALO == SUBLANES
    tm = min(tm, l)
    r = tm // SUBLANES
    last = l // SUBLANES - 1
    row = pl.BlockSpec((1, d), lambda i: (0, 0))
    return pl.pallas_call(
        functools.partial(_pool_kernel, seq=l, windows=POOL_WINDOWS),
        grid=(l // tm,),
        in_specs=[
            pl.BlockSpec((tm, d), lambda i: (i, 0)),
            pl.BlockSpec((SUBLANES, d), lambda i: (jnp.maximum(i * r - 1, 0), 0)),
            pl.BlockSpec((SUBLANES, d), lambda i: (jnp.minimum((i + 1) * r, last), 0)),
            row, row, row,
            pl.BlockSpec((ng, pd, pd), lambda i: (0, 0, 0)),
            pl.BlockSpec((ng, 1, pd), lambda i: (0, 0, 0)),
            row, row,
        ],
        out_specs=pl.BlockSpec((tm, d), lambda i: (i, 0)),
        out_shape=_sds((l, d), F32),
        compiler_params=_params("parallel"),
        name="pool_mixer_residual",
    )(x, x, x, norm_g.reshape(1, d), sc, sh, w.astype(BF16), b.reshape(ng, 1, pd), scale.reshape(1, d), gate)


def kernel(x, c, ctx, c_ctx, norm_mix_g, norm_ffn_g, mod_w, mod_b, in_w, in_b, hy_conv_w, hy_conv_b, hy_w1, hy_b1, hy_w2, hy_b2, hy_w3, hy_b3, hy_freq, hy_w4, hy_skip, hy_norm_g, hg_lb_logits, hg_norm_g, out_w, pool_w, pool_b, pool_scale, ffn_up_w, ffn_conv_w, ffn_conv_b, ffn_down_w, final_norm_g):
    batch, _, d = x.shape
    depth = mod_w.shape[0]
    hy_d = hy_norm_g.shape[1]
    hg_d = hg_norm_g.shape[1]
    hg_q0 = 3 * hy_d
    hg_i0 = hg_q0 + hg_d
    hg_heads = hg_d // HG_EXPAND
    outs = []
    for bi in range(batch):
        xs = x[bi]
        cvecs = jnp.zeros((SUBLANES, d), F32).at[0].set(c[bi]).at[1].set(c_ctx)
        for l in range(depth):
            mod = _adaln(cvecs, mod_w, mod_b, l)
            sh1, sc1, g1, sh2, sc2, g2 = [mod[0:1, k * d:(k + 1) * d] for k in range(N_MOD)]
            if l % 2 == 0:
                e = l // 2
                in_w_bf = in_w[e].astype(BF16)
                h = _norm_mod(xs, norm_mix_g[l], sc1, sh1)
                proj = _matmul_bias(h, in_w_bf, in_b[e])
                hc = _norm_mod(ctx[bi], norm_mix_g[l], mod[1:2, d:2 * d], mod[1:2, 0:d])
                ctx_proj = _matmul_bias(hc, in_w_bf[:, hg_i0:hg_i0 + 3 * hg_d], in_b[e, hg_i0:hg_i0 + 3 * hg_d])
                filt = (hy_w1[e], hy_b1[e], hy_w2[e], hy_b2[e], hy_w3[e], hy_b3[e], hy_freq[e], hy_w4[e])
                y_hy = _hyena_mixer(proj, hy_conv_w[e], hy_conv_b[e], filt, hy_skip[e], hy_norm_g[e], hy_d)
                y_hg = _hgrn_mixer(proj, hg_q0, ctx_proj, hg_lb_logits, e, hg_norm_g[e], hg_heads)
                xs = _outproj(y_hy, y_hg, out_w[e].astype(BF16), xs, g1)
            else:
                od = l // 2
                xs = _pool_layer(xs, norm_mix_g[l], sc1, sh1, pool_w[od], pool_b[od], pool_scale[od], g1)
            h = _norm_mod(xs, norm_ffn_g[l], sc2, sh2)
            xs = _conv_ffn(h, xs, g2, ffn_up_w[l].astype(BF16), ffn_conv_w[l], ffn_conv_b[l],
                           ffn_down_w[l].astype(BF16), final_norm_g if l == depth - 1 else None, GRID_W)
        outs.append(xs[None])
    return outs[0] if batch == 1 else jnp.concatenate(outs, axis=0)
```

```python
import functools
import math

import numpy as np
import jax
import jax.numpy as jnp
from jax import lax
from jax.experimental import pallas as pl
from jax.experimental.pallas import tpu as pltpu

F32 = jnp.float32
BF16 = jnp.bfloat16
U32 = jnp.uint32
EPS = 1e-6

LANES = 128
SUBLANES = 8
VMEM_LIMIT_BYTES = 56 * 1024 * 1024

GRID_W = 64
N_MOD = 6
HY_HEADS = 8
HY_ORDER = 2
HY_EMB = 33
HY_DECAY_TARGET = 1e-2
HY_FAST_PCT = 0.3
HY_SLOW_PCT = 1.5
HG_EXPAND = 128
HG_CHUNK = 64
POOL_WINDOWS = (2, 4, 8, 16)
POOL_HALO = 8
FFT_B = 128
FFN_ROWS = 1024


def _params(*sem):
    return pltpu.CompilerParams(dimension_semantics=sem, vmem_limit_bytes=VMEM_LIMIT_BYTES)


def _sds(shape, dtype):
    return jax.ShapeDtypeStruct(shape, dtype)


def _dot(a, b):
    return jnp.dot(a, b, preferred_element_type=F32)


def _adaln_kernel(c_ref, w_ref, b_ref, o_ref):
    c = c_ref[...]
    s = (c * jax.nn.sigmoid(c)).astype(BF16)
    o_ref[...] = _dot(s, w_ref[...].astype(BF16)) + b_ref[...]


def _adaln(cvecs, mod_w, mod_b, layer, tn=1024):
    _, d, n = mod_w.shape
    tn = min(tn, n)
    return pl.pallas_call(
        _adaln_kernel,
        grid=(n // tn,),
        in_specs=[
            pl.BlockSpec((SUBLANES, d), lambda j: (0, 0)),
            pl.BlockSpec((None, d, tn), lambda j: (layer, 0, j)),
            pl.BlockSpec((None, 1, tn), lambda j: (layer, 0, j)),
        ],
        out_specs=pl.BlockSpec((SUBLANES, tn), lambda j: (0, j)),
        out_shape=_sds((SUBLANES, n), F32),
        compiler_params=_params("arbitrary"),
        name="adaln",
    )(cvecs, mod_w, mod_b.reshape(mod_b.shape[0], 1, n))


def _rms_mod(x, g, sc, sh):
    y = x * lax.rsqrt(jnp.mean(x * x, axis=-1, keepdims=True) + EPS) * g
    return y * (1.0 + sc) + sh


def _norm_mod_kernel(x_ref, g_ref, sc_ref, sh_ref, o_ref):
    o_ref[...] = _rms_mod(x_ref[...], g_ref[...], sc_ref[...], sh_ref[...]).astype(o_ref.dtype)


def _norm_mod(x, g, sc, sh, tm=512):
    m, d = x.shape
    tm = min(tm, m)
    row = pl.BlockSpec((1, d), lambda i: (0, 0))
    return pl.pallas_call(
        _norm_mod_kernel,
        grid=(m // tm,),
        in_specs=[pl.BlockSpec((tm, d), lambda i: (i, 0)), row, row, row],
        out_specs=pl.BlockSpec((tm, d), lambda i: (i, 0)),
        out_shape=_sds((m, d), BF16),
        compiler_params=_params("parallel"),
        name="norm_mod",
    )(x, g.reshape(1, d), sc, sh)


def _norm_matmul_kernel(x_ref, g_ref, sc_ref, sh_ref, w_ref, b_ref, o_ref, h_ref):
    @pl.when(pl.program_id(1) == 0)
    def _():
        h_ref[...] = _rms_mod(x_ref[...], g_ref[...], sc_ref[...], sh_ref[...]).astype(h_ref.dtype)

    o_ref[...] = _dot(h_ref[...], w_ref[...]) + b_ref[...]


def _norm_matmul(x, g, sc, sh, w, b, tm=1024, tn=1024):
    m, k = x.shape
    n = w.shape[1]
    tm, tn = min(tm, m), min(tn, n)
    row = pl.BlockSpec((1, k), lambda i, j: (0, 0))
    return pl.pallas_call(
        _norm_matmul_kernel,
        grid=(m // tm, n // tn),
        in_specs=[
            pl.BlockSpec((tm, k), lambda i, j: (i, 0)), row, row, row,
            pl.BlockSpec((k, tn), lambda i, j: (0, j)),
            pl.BlockSpec((1, tn), lambda i, j: (0, j)),
        ],
        out_specs=pl.BlockSpec((tm, tn), lambda i, j: (i, j)),
        out_shape=_sds((m, n), F32),
        scratch_shapes=[pltpu.VMEM((tm, k), BF16)],
        compiler_params=_params("parallel", "arbitrary"),
        name="norm_matmul",
    )(x, g.reshape(1, k), sc, sh, w, b.reshape(1, n))


def _dwconv1d_kernel(xm_ref, xp_ref, xn_ref, w_ref, b_ref, o_ref):
    i = pl.program_id(0)
    x = xm_ref[...]
    tm = x.shape[0]
    prev_row = jnp.where(i > 0, xp_ref[SUBLANES - 1:SUBLANES, :], 0.0)
    next_row = jnp.where(i < pl.num_programs(0) - 1, xn_ref[0:1, :], 0.0)
    row = lax.broadcasted_iota(jnp.int32, x.shape, 0)
    x_m1 = jnp.where(row == 0, prev_row, pltpu.roll(x, 1, 0))
    x_p1 = jnp.where(row == tm - 1, next_row, pltpu.roll(x, tm - 1, 0))
    o_ref[...] = x_m1 * w_ref[0:1, :] + x * w_ref[1:2, :] + x_p1 * w_ref[2:3, :] + b_ref[...]


def _dwconv1d(proj, w, b, parts, c, tm=512):
    l = proj.shape[0]
    tm = min(tm, l)
    r = tm // SUBLANES
    last = l // SUBLANES - 1
    return pl.pallas_call(
        _dwconv1d_kernel,
        grid=(l // tm, parts),
        in_specs=[
            pl.BlockSpec((tm, c), lambda i, j: (i, j)),
            pl.BlockSpec((SUBLANES, c), lambda i, j: (jnp.maximum(i * r - 1, 0), j)),
            pl.BlockSpec((SUBLANES, c), lambda i, j: (jnp.minimum((i + 1) * r, last), j)),
            pl.BlockSpec((3, c), lambda i, j: (0, j)),
            pl.BlockSpec((1, c), lambda i, j: (0, j)),
        ],
        out_specs=pl.BlockSpec((None, tm, c), lambda i, j: (j, i, 0)),
        out_shape=_sds((parts, l, c), F32),
        compiler_params=_params("parallel", "parallel"),
        name="hyena_dwconv1d",
    )(proj, proj, proj, w, b.reshape(1, parts * c))


def _hyena_pos_features(l):
    t = jnp.linspace(0.0, 1.0, l, dtype=F32)[:, None]
    bands = (HY_EMB - 1) // 2
    f = jnp.linspace(1e-4, bands - 1, bands, dtype=F32)[None, :]
    w = (2.0 * math.pi / l) * jnp.arange(l, dtype=F32)[:, None]
    return jnp.concatenate([t, jnp.cos(f * w), -jnp.sin(f * w)], axis=-1)


def _stage2_matrices():
    b = FFT_B
    angb = 2.0 * np.pi * np.outer(np.arange(b), np.arange(b)) / b
    cb, sb = np.cos(angb), np.sin(angb)
    m_fwd = np.block([[cb, sb], [-sb, cb]])
    m_inv = np.block([[cb, -sb], [sb, cb]])
    as_bf16 = lambda m: jnp.asarray(m, dtype=F32).astype(BF16)
    return as_bf16(m_fwd), as_bf16(m_inv)


def _stage1_matrices(l):
    b = FFT_B
    a = l // b
    na = 2 * a
    n = 2 * l
    kap = jnp.arange(na, dtype=jnp.int32)[None, :, None]
    ai = jnp.arange(na, dtype=jnp.int32)[None, None, :]
    bi = jnp.arange(b, dtype=jnp.int32)[:, None, None]
    ang = ((kap * (ai * b + bi)) % n).astype(F32) * (2.0 * math.pi / n)
    cos, sin = jnp.cos(ang), jnp.sin(ang)
    g1 = jnp.concatenate([cos, -sin], axis=1).astype(BF16)
    inv = lambda t: t[:, :, :a].transpose(0, 2, 1) * (1.0 / n)
    g3 = jnp.concatenate([inv(cos), -inv(sin)], axis=2).astype(BF16)
    return g1, g3


def _cmul(ar, ai, br, bi):
    return ar * br - ai * bi, ar * bi + ai * br


def _pack_complex(re, im):
    r = lax.bitcast_convert_type(re.astype(BF16).astype(F32), U32)
    i = lax.bitcast_convert_type(im.astype(BF16).astype(F32), U32)
    return r | (i >> 16)


def _unpack_complex(w):
    re = lax.bitcast_convert_type(w & jnp.uint32(0xFFFF0000), F32)
    im = lax.bitcast_convert_type(w << 16, F32)
    return re, im


def _stacked_bf16(w):
    re, im = _unpack_complex(w)
    return jnp.concatenate([re, im], axis=0).astype(BF16)


def _strided_rows(chunk_refs, s, rows, stride):
    flat = [r.reshape(rows * stride, LANES) for r in chunk_refs]
    return jnp.concatenate([r[pl.ds(s, rows, stride=stride), :] for r in flat], axis=-1)


def _pack_stacked(y):
    half = y.shape[0] // 2
    return _pack_complex(y[:half], y[half:])


def _fft_stage1_kernel(*refs, segs, nq, strided):
    f_ref, x_refs, o_ref = refs[0], refs[1:1 + nq], refs[1 + nq]
    ka = f_ref.shape[2]
    for s in range(segs):
        x = _strided_rows(x_refs, s, ka, segs) if strided else x_refs[0][s]
        o_ref[s] = _pack_stacked(_dot(f_ref[s], x.astype(BF16)))


def _fft_stage1(x4, part, g1, strided, cb=512):
    if strided:
        _, ka, b, c = x4.shape
    else:
        _, b, ka, c = x4.shape
    rows = g1.shape[1]
    segs = SUBLANES
    cb = min(cb, c)
    if strided:
        nq = cb // LANES
        x_specs = [pl.BlockSpec((None, ka, segs, LANES), functools.partial(lambda j, jc, q: (part, 0, j, jc * nq + q), q=q))
                   for q in range(nq)]
    else:
        nq = 1
        x_specs = [pl.BlockSpec((None, segs, ka, cb), lambda j, jc: (part, j, 0, jc))]
    return pl.pallas_call(
        functools.partial(_fft_stage1_kernel, segs=segs, nq=nq, strided=strided),
        grid=(b // segs, c // cb),
        in_specs=[pl.BlockSpec((segs, rows, ka), lambda j, jc: (j, 0, 0))] + x_specs,
        out_specs=pl.BlockSpec((segs, rows // 2, cb), lambda j, jc: (j, 0, jc)),
        out_shape=_sds((b, rows // 2, c), U32),
        compiler_params=_params("parallel", "parallel"),
        name="fft_stage1",
    )(g1, *([x4] * nq))


def _fft_filter_stage1_kernel(z_ref, w1_ref, b1_ref, w2_ref, b2_ref, w3_ref, b3_ref, fr_ref, w4_ref, dl_ref, f_ref,
                              o_ref, h_ref, *, seq, segs):
    a_n = z_ref.shape[1]
    inv = 1.0 / (seq - 1)

    @pl.when(pl.program_id(1) == 0)
    def _():
        for s in range(segs):
            h = jnp.sin(fr_ref[0:1, :] * (_dot(z_ref[s], w1_ref[...]) + b1_ref[...]))
            h = jnp.sin(fr_ref[1:2, :] * (_dot(h.astype(BF16), w2_ref[...]) + b2_ref[...]))
            h = jnp.sin(fr_ref[2:3, :] * (_dot(h.astype(BF16), w3_ref[...]) + b3_ref[...]))
            h_ref[s] = h.astype(BF16)

    for s in range(segs):
        b = pl.program_id(0) * segs + s
        n = (lax.broadcasted_iota(jnp.int32, (a_n, 1), 0) * FFT_B + b).astype(F32)
        e_fwd = jnp.exp(-(n * inv) * dl_ref[...])
        e_rev = jnp.where(n == 0.0, 0.0, jnp.exp(-((seq - n) * inv) * dl_ref[...]))
        h = h_ref[s]
        for o in range(HY_ORDER):
            k = jnp.concatenate([_dot(h, w4_ref[2 * o]) * e_fwd, _dot(h, w4_ref[2 * o + 1]) * e_rev], axis=0)
            o_ref[o, s] = _pack_stacked(_dot(f_ref[s], k.astype(BF16)))


def _block_diag2(w):
    z = jnp.zeros_like(w)
    return jnp.concatenate([jnp.concatenate([w, z], axis=1), jnp.concatenate([z, w], axis=1)], axis=0)


def _fft_filter_stage1(l, c, w1, b1, w2, b2, w3, b3, freq, w4, g1, cb=256):
    fw = w1.shape[1]
    emb = 64
    a = l // FFT_B
    rows = g1.shape[1]
    segs = SUBLANES
    cb = min(cb, c)
    z = _hyena_pos_features(l)
    z_rev = jnp.concatenate([z[:1], z[:0:-1]], axis=0)
    pad = lambda t: jnp.pad(t, ((0, 0), (0, emb - HY_EMB)))
    z2 = jnp.concatenate([pad(z), pad(z_rev)], axis=1).astype(BF16)
    z2 = z2.reshape(a, FFT_B, 2 * emb).transpose(1, 0, 2)
    both = lambda v: jnp.concatenate([v, v]).reshape(1, 2 * fw)
    w1d = _block_diag2(jnp.pad(w1, ((0, emb - HY_EMB), (0, 0)))).astype(BF16)
    w4r = w4.reshape(fw, HY_ORDER * 2, c).transpose(1, 0, 2)
    zeros = jnp.zeros_like(w4r)
    fwd_dir = (jnp.arange(HY_ORDER * 2) % 2 == 0)[:, None, None]
    w4d = jnp.concatenate([jnp.where(fwd_dir, w4r, zeros), jnp.where(fwd_dir, zeros, w4r)], axis=1).astype(BF16)
    max_decay = math.log(HY_DECAY_TARGET) / HY_FAST_PCT
    min_decay = math.log(HY_DECAY_TARGET) / HY_SLOW_PCT
    deltas = jnp.abs(jnp.linspace(min_decay, max_decay, c, dtype=F32)).reshape(1, c)
    full = lambda shape: pl.BlockSpec(shape, lambda j, jc: (0,) * len(shape))
    return pl.pallas_call(
        functools.partial(_fft_filter_stage1_kernel, seq=l, segs=segs),
        grid=(FFT_B // segs, c // cb),
        in_specs=[
            pl.BlockSpec((segs, a, 2 * emb), lambda j, jc: (j, 0, 0)),
            full((2 * emb, 2 * fw)), full((1, 2 * fw)), full((2 * fw, 2 * fw)), full((1, 2 * fw)),
            full((2 * fw, 2 * fw)), full((1, 2 * fw)), full((3, 2 * fw)),
            pl.BlockSpec((HY_ORDER * 2, 2 * fw, cb), lambda j, jc: (0, 0, jc)),
            pl.BlockSpec((1, cb), lambda j, jc: (0, jc)),
            pl.BlockSpec((segs, rows, 2 * a), lambda j, jc: (j, 0, 0)),
        ],
        out_specs=pl.BlockSpec((HY_ORDER, segs, rows // 2, cb), lambda j, jc: (0, j, 0, jc)),
        out_shape=_sds((HY_ORDER, FFT_B, rows // 2, c), U32),
        scratch_shapes=[pltpu.VMEM((segs, a, 2 * fw), BF16)],
        compiler_params=_params("parallel", "arbitrary"),
        name="fft_filter_stage1",
    )(z2, w1d, both(b1), _block_diag2(w2).astype(BF16), both(b2), _block_diag2(w3).astype(BF16), both(b3),
      jnp.concatenate([freq, freq], axis=1), w4d, deltas, g1)


def _fft_mid_kernel(*refs, kq, nc):
    mf_ref, mi_ref = refs[0], refs[1]
    y_refs, k_refs = refs[2:2 + nc], refs[2 + nc:2 + 2 * nc]
    o_ref = refs[2 + 2 * nc]
    b = mf_ref.shape[0] // 2
    for q in range(kq):
        x = _dot(mf_ref[...], _stacked_bf16(_strided_rows(y_refs, q, b, kq)))
        k = _dot(mf_ref[...], _stacked_bf16(_strided_rows(k_refs, q, b, kq)))
        pr, pi = _cmul(x[:b], x[b:], k[:b], k[b:])
        o_ref[q] = _pack_stacked(_dot(mi_ref[...], jnp.concatenate([pr, pi], axis=0).astype(BF16)))


def _fft_mid(y3, yk4, order, m_fwd, m_inv):
    b, na, c = y3.shape
    kq = SUBLANES
    nc = c // LANES
    mat = pl.BlockSpec((2 * b, 2 * b), lambda i: (0, 0))
    y_specs = [pl.BlockSpec((b, kq, LANES), functools.partial(lambda i, q: (0, i, q), q=q)) for q in range(nc)]
    k_specs = [pl.BlockSpec((None, b, kq, LANES), functools.partial(lambda i, q: (order, 0, i, q), q=q))
               for q in range(nc)]
    return pl.pallas_call(
        functools.partial(_fft_mid_kernel, kq=kq, nc=nc),
        grid=(na // kq,),
        in_specs=[mat, mat] + y_specs + k_specs,
        out_specs=pl.BlockSpec((kq, b, c), lambda i: (i, 0, 0)),
        out_shape=_sds((na, b, c), U32),
        compiler_params=_params("parallel"),
        name="fft_mid",
    )(m_fwd, m_inv, *([y3] * nc), *([yk4] * nc))


def _fft_stage3_kernel(*refs, segs, nq, u_strided, head_dim):
    f_ref, v_refs = refs[0], refs[1:1 + nq]
    nu = nq if u_strided else 1
    u_refs, g_refs = refs[1 + nq:1 + nq + nu], refs[1 + nq + nu:1 + 2 * nq + nu]
    sk_ref, ng_ref, o_ref = refs[1 + 2 * nq + nu:]
    a = f_ref.shape[1]
    rows = f_ref.shape[2] // 2
    c = o_ref.shape[-1]
    for s in range(segs):
        y = _dot(f_ref[s], _stacked_bf16(_strided_rows(v_refs, s, rows, segs)))
        u = _strided_rows(u_refs, s, a, segs) if u_strided else u_refs[0][s]
        z = _strided_rows(g_refs, s, a, segs) * (y + u * sk_ref[...])
        if head_dim:
            heads = [z[:, h:h + head_dim] for h in range(0, c, head_dim)]
            heads = [zh * lax.rsqrt(jnp.mean(zh * zh, axis=-1, keepdims=True) + EPS) for zh in heads]
            z = jnp.concatenate(heads, axis=-1) * ng_ref[...]
        o_ref[s] = z


def _fft_stage3(v3, f3, u4, u_part, u_strided, g4, g_part, skip, norm_g, head_dim, cb=512):
    a = f3.shape[1]
    rows, b, c = v3.shape
    segs = SUBLANES
    cb = min(cb, c)
    nq = cb // LANES
    chunk = lambda shape, idx: [pl.BlockSpec(shape, functools.partial(idx, q=q)) for q in range(nq)]
    v_specs = chunk((rows, segs, LANES), lambda j, jc, q: (0, j, jc * nq + q))
    nat = lambda part: chunk((None, a, segs, LANES), lambda j, jc, q: (part, 0, j, jc * nq + q))
    u_specs = nat(u_part) if u_strided else [pl.BlockSpec((None, segs, a, cb), lambda j, jc: (u_part, j, 0, jc))]
    vec = pl.BlockSpec((1, cb), lambda j, jc: (0, jc))
    return pl.pallas_call(
        functools.partial(_fft_stage3_kernel, segs=segs, nq=nq, u_strided=u_strided, head_dim=head_dim),
        grid=(b // segs, c // cb),
        in_specs=[pl.BlockSpec((segs, a, 2 * rows), lambda j, jc: (j, 0, 0))] + v_specs + u_specs + nat(g_part)
        + [vec, vec],
        out_specs=pl.BlockSpec((segs, a, cb), lambda j, jc: (j, 0, jc)),
        out_shape=_sds((b, a, c), F32),
        compiler_params=_params("parallel", "parallel"),
        name="fft_stage3",
    )(f3, *([v3] * nq), *([u4] * (nq if u_strided else 1)), *([g4] * nq), skip.reshape(1, c), norm_g.reshape(1, c))


def _hyena_mixer(proj, conv_w, conv_b, filt, skip, norm_g, c):
    l = proj.shape[0]
    a = l // FFT_B
    hyc4 = _dwconv1d(proj, conv_w, conv_b, 3, c).reshape(3, a, FFT_B, c)
    m_fwd, m_inv = _stage2_matrices()
    g1, f3 = _stage1_matrices(l)
    yk = _fft_filter_stage1(l, c, *filt, g1)
    z4, part, strided = hyc4, 0, True
    for o in range(HY_ORDER):
        y3 = _fft_stage1(z4, part, g1, strided)
        v3 = _fft_mid(y3, yk, o, m_fwd, m_inv)
        last = o == HY_ORDER - 1
        z = _fft_stage3(v3, f3, z4, part, strided, hyc4, 1 + o, skip[o], norm_g, c // HY_HEADS if last else 0)
        z4, part, strided = z[None], 0, False
    return z


def _lower_bound(lg_ref, slot):
    lg = lg_ref[...]
    e = jnp.exp(lg - jnp.max(lg, axis=0, keepdims=True))
    sm = e / jnp.sum(e, axis=0, keepdims=True)
    return jnp.sum(sm[:slot + 1], axis=0, keepdims=True)


def _prefix_sum_rows(x):
    t = x.shape[0]
    row = lax.broadcasted_iota(jnp.int32, x.shape, 0)
    s = 1
    while s < t:
        x = x + jnp.where(row >= s, pltpu.roll(x, s, 0), 0.0)
        s *= 2
    return x


def _gla_kernel(*refs, slot, reverse, nchunk, chunk, hpb, with_out, combine):
    it = iter(refs)
    i_ref, f_ref, lg_ref, s0_ref = next(it), next(it), next(it), next(it)
    q_ref = next(it) if with_out else None
    ofw_ref, g_ref, ng_ref = (next(it), next(it), next(it)) if combine else (None, None, None)
    o_ref = next(it) if with_out else None
    sf_ref, st_ref = next(it), next(it)
    ke = HG_EXPAND

    @pl.when(pl.program_id(1) == 0)
    def _():
        st_ref[...] = s0_ref[...]

    lb = _lower_bound(lg_ref, slot)
    row = lax.broadcasted_iota(jnp.int32, (chunk, chunk), 0)
    col = lax.broadcasted_iota(jnp.int32, (chunk, chunk), 1)
    keep = (col >= row) if reverse else (col <= row)
    nt = (((1,), (1,)), ((), ()))
    tn = (((0,), (0,)), ((), ()))
    st = [st_ref[h] for h in range(hpb)]
    order = range(nchunk - 1, -1, -1) if reverse else range(nchunk)
    for ci in order:
        rows = slice(ci * chunk, (ci + 1) * chunk)
        f = lb + (1.0 - lb) * jax.nn.sigmoid(f_ref[rows, :])
        logf = jnp.log(f)
        key = 1.0 - f
        b = _prefix_sum_rows(logf)
        total = b[chunk - 1:chunk, :]
        if reverse:
            b = total - b + logf
        v = i_ref[rows, :].astype(BF16)
        k_state = (key * jnp.exp(total - b)).astype(BF16)
        decay = jnp.exp(total)
        if with_out:
            q = q_ref[rows, :]
            q_in = (q * jax.nn.sigmoid(q) * jnp.exp(b)).astype(BF16)
            k_intra = (key * jnp.exp(-b)).astype(BF16)
        outs = []
        for h in range(hpb):
            hl = slice(h * ke, (h + 1) * ke)
            if with_out:
                att = lax.dot_general(q_in[:, hl], k_intra[:, hl], nt, preferred_element_type=F32)
                att = jnp.where(keep, att, 0.0).astype(BF16)
                o = _dot(att, v[:, hl]) + lax.dot_general(q_in[:, hl], st[h].astype(BF16), nt,
                                                           preferred_element_type=F32)
                if combine:
                    o = o + ofw_ref[rows, hl]
                    o = o * lax.rsqrt(jnp.mean(o * o, axis=-1, keepdims=True) + EPS)
                outs.append(o)
            st[h] = st[h] * decay[:, hl] + lax.dot_general(v[:, hl], k_state[:, hl], tn, preferred_element_type=F32)
        if with_out:
            o = jnp.concatenate(outs, axis=-1)
            if combine:
                g = g_ref[rows, :]
                o = o * ng_ref[...] * (g * jax.nn.sigmoid(g))
            o_ref[rows, :] = o.astype(o_ref.dtype)
    for h in range(hpb):
        st_ref[h] = st[h]

    @pl.when(pl.program_id(1) == pl.num_programs(1) - 1)
    def _():
        for h in range(hpb):
            sf_ref[h] = st[h]


def _gla(src, cols, lb_logits, slot, s0, reverse, with_out, fwd_out=None, norm_g=None, out_dtype=F32, tb=512, hpb=2):
    l = src.shape[0]
    heads = s0.shape[0]
    ke = HG_EXPAND
    tb = min(tb, l)
    nblk = l // tb
    if heads % hpb or any(off % hpb for off in cols.values()):
        hpb = 1
    wd = hpb * ke
    combine = fwd_out is not None
    blk = (lambda j: nblk - 1 - j) if reverse else (lambda j: j)
    col = lambda name: pl.BlockSpec((tb, wd), lambda h, j: (blk(j), cols[name] // hpb + h))
    head_rows = pl.BlockSpec((tb, wd), lambda h, j: (blk(j), h))
    state = pl.BlockSpec((hpb, ke, ke), lambda h, j: (h, 0, 0))
    in_specs = [col("i"), col("f"), pl.BlockSpec((lb_logits.shape[0], wd), lambda h, j: (0, h)), state]
    args = [src, src, lb_logits, s0]
    out_specs, out_shape = [], []
    if with_out:
        in_specs.append(col("q"))
        args.append(src)
        out_specs.append(head_rows)
        out_shape.append(_sds((l, heads * ke), out_dtype))
    if combine:
        in_specs += [head_rows, col("g"), pl.BlockSpec((1, wd), lambda h, j: (0, h))]
        args += [fwd_out, src, norm_g.reshape(1, heads * ke)]
    out_specs.append(state)
    out_shape.append(_sds((heads, ke, ke), F32))
    res = pl.pallas_call(
        functools.partial(_gla_kernel, slot=slot, reverse=reverse, nchunk=tb // HG_CHUNK, chunk=HG_CHUNK, hpb=hpb,
                          with_out=with_out, combine=combine),
        grid=(heads // hpb, nblk),
        in_specs=in_specs,
        out_specs=out_specs,
        out_shape=out_shape,
        scratch_shapes=[pltpu.VMEM((hpb, ke, ke), F32)],
        compiler_params=_params("parallel", "arbitrary"),
        name="hgrn_gla",
    )(*args)
    return res if with_out else (None, res[0])


def _hgrn_mixer(proj, q0, ctx_proj, lb_logits, slot, norm_g, heads):
    qb = q0 // HG_EXPAND
    zeros = jnp.zeros((heads, HG_EXPAND, HG_EXPAND), F32)
    _, s0_fw = _gla(ctx_proj, {"i": 0, "f": heads}, lb_logits, slot, zeros, False, False)
    _, s0_bw = _gla(ctx_proj, {"i": 0, "f": 2 * heads}, lb_logits, slot, zeros, True, False)
    cols = {"q": qb, "i": qb + heads, "f": qb + 2 * heads, "g": qb + 4 * heads}
    o_fw, _ = _gla(proj, cols, lb_logits, slot, s0_fw, False, True)
    cols["f"] = qb + 3 * heads
    y, _ = _gla(proj, cols, lb_logits, slot, s0_bw, True, True, fwd_out=o_fw, norm_g=norm_g, out_dtype=BF16)
    return y


def _outproj_kernel(*refs, na, nc):
    a_refs = refs[:nc]
    b_ref, w_ref, x_ref, g_ref, o_ref = refs[nc:]
    rows = a_refs[0].shape[0]
    k1 = nc * LANES
    ya = jnp.concatenate([_strided_rows(a_refs, s, rows, na) for s in range(na)], axis=0).astype(BF16)
    y = _dot(ya, w_ref[:k1, :]) + _dot(b_ref[...], w_ref[k1:, :])
    o_ref[...] = x_ref[...] + g_ref[...] * y


def _outproj(ya3, yb, w, x, gate, tn=1024):
    m, d = x.shape
    b, a, k1 = ya3.shape
    k2 = yb.shape[1]
    na = min(SUBLANES, a)
    tm = na * b
    tn = min(tn, d)
    nc = k1 // LANES
    a_specs = [pl.BlockSpec((b, na, LANES), functools.partial(lambda j, i, q: (0, i, q), q=q)) for q in range(nc)]
    return pl.pallas_call(
        functools.partial(_outproj_kernel, na=na, nc=nc),
        grid=(d // tn, m // tm),
        in_specs=a_specs + [
            pl.BlockSpec((tm, k2), lambda j, i: (i, 0)),
            pl.BlockSpec((k1 + k2, tn), lambda j, i: (0, j)),
            pl.BlockSpec((tm, tn), lambda j, i: (i, j)),
            pl.BlockSpec((1, tn), lambda j, i: (0, j)),
        ],
        out_specs=pl.BlockSpec((tm, tn), lambda j, i: (i, j)),
        out_shape=_sds((m, d), F32),
        compiler_params=_params("parallel", "parallel"),
        name="outproj_residual",
    )(*([ya3] * nc), yb, w, x, gate)


def _ffn_kernel(h_hbm, wa_ref, wu_ref, cw_ref, cb_ref, wd_ref, x_hbm, g_ref, fg_ref, o_ref,
                hext_ref, x_ref, act_ref, sem, *, gw, final):
    i, j = pl.program_id(0), pl.program_id(1)
    ni, nj = pl.num_programs(0), pl.num_programs(1) - 1
    tm = o_ref.shape[0]
    n = tm + 2 * gw
    row0 = i * tm

    def h_copy(src_row, dst_row, rows, k):
        return pltpu.make_async_copy(h_hbm.at[pl.ds(src_row, rows)], hext_ref.at[pl.ds(dst_row, rows)], sem.at[k])

    x_copy = pltpu.make_async_copy(x_hbm.at[pl.ds(row0, tm)], x_ref, sem.at[3])
    has_prev, has_next = i > 0, i < ni - 1

    def up_part():
        a = _dot(hext_ref[...], wa_ref[...])
        u = _dot(hext_ref[gw:gw + tm, :], wu_ref[...])
        col = lax.broadcasted_iota(jnp.int32, a.shape, 0) % gw
        a_m1 = jnp.where(col == 0, 0.0, pltpu.roll(a, 1, 0))
        a_p1 = jnp.where(col == gw - 1, 0.0, pltpu.roll(a, n - 1, 0))
        conv = cb_ref[...]
        for di in range(3):
            rows = slice(di * gw, di * gw + tm)
            conv = conv + (a_m1[rows] * cw_ref[3 * di:3 * di + 1, :] + a[rows] * cw_ref[3 * di + 1:3 * di + 2, :]
                           + a_p1[rows] * cw_ref[3 * di + 2:3 * di + 3, :])
        gelu = 0.5 * conv * (1.0 + lax.erf(conv * (1.0 / math.sqrt(2.0))))
        act_ref[j % 2] = (gelu * u).astype(BF16)

    def down_part():
        o_ref[...] += _dot(act_ref[(j + 1) % 2], wd_ref[...])

    @pl.when(j == 0)
    def _():
        h_copy(row0, gw, tm, 0).start()
        x_copy.start()

        @pl.when(has_prev)
        def _():
            h_copy(row0 - gw, 0, gw, 1).start()

        @pl.when(has_next)
        def _():
            h_copy(row0 + tm, gw + tm, gw, 2).start()

        @pl.when(jnp.logical_not(has_prev))
        def _():
            hext_ref[0:gw, :] = jnp.zeros((gw, hext_ref.shape[1]), hext_ref.dtype)

        @pl.when(jnp.logical_not(has_next))
        def _():
            hext_ref[gw + tm:n, :] = jnp.zeros((gw, hext_ref.shape[1]), hext_ref.dtype)

        o_ref[...] = jnp.zeros_like(o_ref)
        h_copy(row0, gw, tm, 0).wait()

        @pl.when(has_prev)
        def _():
            h_copy(row0 - gw, 0, gw, 1).wait()

        @pl.when(has_next)
        def _():
            h_copy(row0 + tm, gw + tm, gw, 2).wait()

        up_part()

    @pl.when(jnp.logical_and(j > 0, j < nj))
    def _():
        down_part()
        up_part()

    @pl.when(j == nj)
    def _():
        down_part()
        x_copy.wait()
        y = x_ref[...] + g_ref[...] * o_ref[...]
        if final:
            y = y * lax.rsqrt(jnp.mean(y * y, axis=-1, keepdims=True) + EPS) * fg_ref[...]
        o_ref[...] = y


def _conv_ffn(h, x, gate, w_up, conv_w, conv_b, w_down, final_g, gw, tf=512):
    l, d = x.shape
    dff = w_down.shape[0]
    tm, tf = min(FFN_ROWS, l), min(tf, dff)
    nj = dff // tf
    final = final_g is not None
    fg = (final_g if final else jnp.ones((d,), F32)).reshape(1, d)
    hbm = pl.BlockSpec(memory_space=pl.ANY)
    up = lambda j: jnp.minimum(j, nj - 1)
    down = lambda j: jnp.maximum(j - 1, 0)
    return pl.pallas_call(
        functools.partial(_ffn_kernel, gw=gw, final=final),
        grid=(l // tm, nj + 1),
        in_specs=[
            hbm,
            pl.BlockSpec((d, tf), lambda i, j: (0, up(j))),
            pl.BlockSpec((d, tf), lambda i, j: (0, nj + up(j))),
            pl.BlockSpec((9, tf), lambda i, j: (0, up(j))),
            pl.BlockSpec((1, tf), lambda i, j: (0, up(j))),
            pl.BlockSpec((tf, d), lambda i, j: (down(j), 0)),
            hbm,
            pl.BlockSpec((1, d), lambda i, j: (0, 0)),
            pl.BlockSpec((1, d), lambda i, j: (0, 0)),
        ],
        out_specs=pl.BlockSpec((tm, d), lambda i, j: (i, 0)),
        out_shape=_sds((l, d), F32),
        scratch_shapes=[pltpu.VMEM((tm + 2 * gw, d), BF16), pltpu.VMEM((tm, d), F32),
                        pltpu.VMEM((2, tm, tf), BF16), pltpu.SemaphoreType.DMA((4,))],
        compiler_params=_params("arbitrary", "arbitrary"),
        name="conv_ffn",
    )(h, w_up, w_up, conv_w.reshape(9, dff), conv_b.reshape(1, dff), w_down, x, gate, fg)


def _pool_kernel(xm_ref, xp_ref, xn_ref, ng_ref, sc_ref, sh_ref, w_ref, b_ref, ps_ref, g_ref, o_ref, *, seq,
                 windows):
    i = pl.program_id(0)
    tm = xm_ref.shape[0]
    pd = w_ref.shape[1]
    norm = lambda x: _rms_mod(x, ng_ref[...], sc_ref[...], sh_ref[...])
    xm = xm_ref[...]
    hm = norm(xm)
    hp = jnp.where(i > 0, norm(xp_ref[...]), 0.0)
    hn = jnp.where(i < pl.num_programs(0) - 1, norm(xn_ref[...]), 0.0)
    t = i * tm + lax.broadcasted_iota(jnp.int32, (tm, 1), 0)
    n = tm + 2 * POOL_HALO
    for gi, win in enumerate(windows):
        cols = slice(gi * pd, (gi + 1) * pd)
        e = jnp.concatenate([hp[:, cols], hm[:, cols], hn[:, cols]], axis=0)
        p = e + pltpu.roll(e, 1, 0)
        w = 2
        while w < win:
            p = pltpu.roll(p, w // 2, 0) + pltpu.roll(p, n - w // 2, 0)
            w *= 2
        count = (jnp.clip(t + win // 2, 0, seq) - jnp.clip(t - win // 2, 0, seq)).astype(F32)
        mean = p[POOL_HALO:POOL_HALO + tm] / count
        y = _dot((mean - hm[:, cols]).astype(BF16), w_ref[gi]) + b_ref[gi]
        o_ref[:, cols] = xm[:, cols] + g_ref[:, cols] * (y * ps_ref[:, cols])


def _pool_layer(x, norm_g, sc, sh, w, b, scale, gate, tm=512):
    l, d = x.shape
    ng, pd, _ = w.shape
    assert max(POOL_WINDOWS) // 2 <= POOL_HALO == SUBLANES
    tm = min(tm, l)
    r = tm // SUBLANES
    last = l // SUBLANES - 1
    row = pl.BlockSpec((1, d), lambda i: (0, 0))
    return pl.pallas_call(
        functools.partial(_pool_kernel, seq=l, windows=POOL_WINDOWS),
        grid=(l // tm,),
        in_specs=[
            pl.BlockSpec((tm, d), lambda i: (i, 0)),
            pl.BlockSpec((SUBLANES, d), lambda i: (jnp.maximum(i * r - 1, 0), 0)),
            pl.BlockSpec((SUBLANES, d), lambda i: (jnp.minimum((i + 1) * r, last), 0)),
            row, row, row,
            pl.BlockSpec((ng, pd, pd), lambda i: (0, 0, 0)),
            pl.BlockSpec((ng, 1, pd), lambda i: (0, 0, 0)),
            row, row,
        ],
        out_specs=pl.BlockSpec((tm, d), lambda i: (i, 0)),
        out_shape=_sds((l, d), F32),
        compiler_params=_params("parallel"),
        name="pool_mixer_residual",
    )(x, x, x, norm_g.reshape(1, d), sc, sh, w.astype(BF16), b.reshape(ng, 1, pd), scale.reshape(1, d), gate)


def kernel(x, c, ctx, c_ctx, norm_mix_g, norm_ffn_g, mod_w, mod_b, in_w, in_b, hy_conv_w, hy_conv_b, hy_w1, hy_b1, hy_w2, hy_b2, hy_w3, hy_b3, hy_freq, hy_w4, hy_skip, hy_norm_g, hg_lb_logits, hg_norm_g, out_w, pool_w, pool_b, pool_scale, ffn_up_w, ffn_conv_w, ffn_conv_b, ffn_down_w, final_norm_g):
    batch, _, d = x.shape
    depth = mod_w.shape[0]
    hy_d = hy_norm_g.shape[1]
    hg_d = hg_norm_g.shape[1]
    hg_q0 = 3 * hy_d
    hg_i0 = hg_q0 + hg_d
    hg_heads = hg_d // HG_EXPAND
    outs = []
    for bi in range(batch):
        xs = x[bi]
        cvecs = jnp.zeros((SUBLANES, d), F32).at[0].set(c[bi]).at[1].set(c_ctx)
        for l in range(depth):
            mod = _adaln(cvecs, mod_w, mod_b, l)
            sh1, sc1, g1, sh2, sc2, g2 = [mod[0:1, k * d:(k + 1) * d] for k in range(N_MOD)]
            if l % 2 == 0:
                e = l // 2
                in_w_bf = in_w[e].astype(BF16)
                proj = _norm_matmul(xs, norm_mix_g[l], sc1, sh1, in_w_bf, in_b[e])
                ctx_proj = _norm_matmul(ctx[bi], norm_mix_g[l], mod[1:2, d:2 * d], mod[1:2, 0:d],
                                        in_w_bf[:, hg_i0:hg_i0 + 3 * hg_d], in_b[e, hg_i0:hg_i0 + 3 * hg_d])
                filt = (hy_w1[e], hy_b1[e], hy_w2[e], hy_b2[e], hy_w3[e], hy_b3[e], hy_freq[e], hy_w4[e])
                y_hy = _hyena_mixer(proj, hy_conv_w[e], hy_conv_b[e], filt, hy_skip[e], hy_norm_g[e], hy_d)
                y_hg = _hgrn_mixer(proj, hg_q0, ctx_proj, hg_lb_logits, e, hg_norm_g[e], hg_heads)
                xs = _outproj(y_hy, y_hg, out_w[e].astype(BF16), xs, g1)
            else:
                od = l // 2
                xs = _pool_layer(xs, norm_mix_g[l], sc1, sh1, pool_w[od], pool_b[od], pool_scale[od], g1)
            h = _norm_mod(xs, norm_ffn_g[l], sc2, sh2)
            xs = _conv_ffn(h, xs, g2, ffn_up_w[l].astype(BF16), ffn_conv_w[l], ffn_conv_b[l],
                           ffn_down_w[l].astype(BF16), final_norm_g if l == depth - 1 else None, GRID_W)
        outs.append(xs[None])
    return outs[0] if batch == 1 else jnp.concatenate(outs, axis=0)
```

```python
import functools
import math

import numpy as np
import jax
import jax.numpy as jnp
from jax import lax
from jax.experimental import pallas as pl
from jax.experimental.pallas import tpu as pltpu

F32 = jnp.float32
BF16 = jnp.bfloat16
U32 = jnp.uint32
EPS = 1e-6

LANES = 128
SUBLANES = 8
VMEM_LIMIT_BYTES = 56 * 1024 * 1024

GRID_W = 64
N_MOD = 6
HY_HEADS = 8
HY_ORDER = 2
HY_EMB = 33
HY_DECAY_TARGET = 1e-2
HY_FAST_PCT = 0.3
HY_SLOW_PCT = 1.5
HG_EXPAND = 128
HG_CHUNK = 64
POOL_WINDOWS = (2, 4, 8, 16)
POOL_HALO = 8
FFT_B = 128
FFN_ROWS = 1024


def _params(*sem):
    return pltpu.CompilerParams(dimension_semantics=sem, vmem_limit_bytes=VMEM_LIMIT_BYTES)


def _sds(shape, dtype):
    return jax.ShapeDtypeStruct(shape, dtype)


def _dot(a, b):
    return jnp.dot(a, b, preferred_element_type=F32)


def _adaln_kernel(c_ref, w_ref, b_ref, o_ref):
    c = c_ref[...]
    s = (c * jax.nn.sigmoid(c)).astype(BF16)
    o_ref[...] = _dot(s, w_ref[...].astype(BF16)) + b_ref[...]


def _adaln(cvecs, mod_w, mod_b, layer, tn=1024):
    _, d, n = mod_w.shape
    tn = min(tn, n)
    return pl.pallas_call(
        _adaln_kernel,
        grid=(n // tn,),
        in_specs=[
            pl.BlockSpec((SUBLANES, d), lambda j: (0, 0)),
            pl.BlockSpec((None, d, tn), lambda j: (layer, 0, j)),
            pl.BlockSpec((None, 1, tn), lambda j: (layer, 0, j)),
        ],
        out_specs=pl.BlockSpec((SUBLANES, tn), lambda j: (0, j)),
        out_shape=_sds((SUBLANES, n), F32),
        compiler_params=_params("arbitrary"),
        name="adaln",
    )(cvecs, mod_w, mod_b.reshape(mod_b.shape[0], 1, n))


def _rms_mod(x, g, sc, sh):
    y = x * lax.rsqrt(jnp.mean(x * x, axis=-1, keepdims=True) + EPS) * g
    return y * (1.0 + sc) + sh


def _norm_mod_kernel(x_ref, g_ref, sc_ref, sh_ref, o_ref):
    o_ref[...] = _rms_mod(x_ref[...], g_ref[...], sc_ref[...], sh_ref[...]).astype(o_ref.dtype)


def _norm_mod(x, g, sc, sh, tm=1024):
    m, d = x.shape
    tm = min(tm, m)
    row = pl.BlockSpec((1, d), lambda i: (0, 0))
    return pl.pallas_call(
        _norm_mod_kernel,
        grid=(m // tm,),
        in_specs=[pl.BlockSpec((tm, d), lambda i: (i, 0)), row, row, row],
        out_specs=pl.BlockSpec((tm, d), lambda i: (i, 0)),
        out_shape=_sds((m, d), BF16),
        compiler_params=_params("parallel"),
        name="norm_mod",
    )(x, g.reshape(1, d), sc, sh)


def _norm_matmul_kernel(x_ref, g_ref, sc_ref, sh_ref, w_ref, b_ref, o_ref, h_ref):
    @pl.when(pl.program_id(1) == 0)
    def _():
        h_ref[...] = _rms_mod(x_ref[...], g_ref[...], sc_ref[...], sh_ref[...]).astype(h_ref.dtype)

    o_ref[...] = _dot(h_ref[...], w_ref[...]) + b_ref[...]


def _norm_matmul(x, g, sc, sh, w, b, tm=1024, tn=1024):
    m, k = x.shape
    n = w.shape[1]
    tm, tn = min(tm, m), min(tn, n)
    row = pl.BlockSpec((1, k), lambda i, j: (0, 0))
    return pl.pallas_call(
        _norm_matmul_kernel,
        grid=(m // tm, n // tn),
        in_specs=[
            pl.BlockSpec((tm, k), lambda i, j: (i, 0)), row, row, row,
            pl.BlockSpec((k, tn), lambda i, j: (0, j)),
            pl.BlockSpec((1, tn), lambda i, j: (0, j)),
        ],
        out_specs=pl.BlockSpec((tm, tn), lambda i, j: (i, j)),
        out_shape=_sds((m, n), F32),
        scratch_shapes=[pltpu.VMEM((tm, k), BF16)],
        compiler_params=_params("parallel", "arbitrary"),
        name="norm_matmul",
    )(x, g.reshape(1, k), sc, sh, w, b.reshape(1, n))


def _dwconv1d_kernel(xm_ref, xp_ref, xn_ref, w_ref, b_ref, o_ref):
    i = pl.program_id(0)
    x = xm_ref[...]
    tm = x.shape[0]
    prev_row = jnp.where(i > 0, xp_ref[SUBLANES - 1:SUBLANES, :], 0.0)
    next_row = jnp.where(i < pl.num_programs(0) - 1, xn_ref[0:1, :], 0.0)
    row = lax.broadcasted_iota(jnp.int32, x.shape, 0)
    x_m1 = jnp.where(row == 0, prev_row, pltpu.roll(x, 1, 0))
    x_p1 = jnp.where(row == tm - 1, next_row, pltpu.roll(x, tm - 1, 0))
    o_ref[...] = x_m1 * w_ref[0:1, :] + x * w_ref[1:2, :] + x_p1 * w_ref[2:3, :] + b_ref[...]


def _dwconv1d(proj, w, b, parts, c, tm=512):
    l = proj.shape[0]
    tm = min(tm, l)
    r = tm // SUBLANES
    last = l // SUBLANES - 1
    return pl.pallas_call(
        _dwconv1d_kernel,
        grid=(l // tm, parts),
        in_specs=[
            pl.BlockSpec((tm, c), lambda i, j: (i, j)),
            pl.BlockSpec((SUBLANES, c), lambda i, j: (jnp.maximum(i * r - 1, 0), j)),
            pl.BlockSpec((SUBLANES, c), lambda i, j: (jnp.minimum((i + 1) * r, last), j)),
            pl.BlockSpec((3, c), lambda i, j: (0, j)),
            pl.BlockSpec((1, c), lambda i, j: (0, j)),
        ],
        out_specs=pl.BlockSpec((None, tm, c), lambda i, j: (j, i, 0)),
        out_shape=_sds((parts, l, c), F32),
        compiler_params=_params("parallel", "parallel"),
        name="hyena_dwconv1d",
    )(proj, proj, proj, w, b.reshape(1, parts * c))


def _hyena_pos_features(l):
    t = jnp.linspace(0.0, 1.0, l, dtype=F32)[:, None]
    bands = (HY_EMB - 1) // 2
    f = jnp.linspace(1e-4, bands - 1, bands, dtype=F32)[None, :]
    w = (2.0 * math.pi / l) * jnp.arange(l, dtype=F32)[:, None]
    return jnp.concatenate([t, jnp.cos(f * w), -jnp.sin(f * w)], axis=-1)


def _stage2_matrices():
    b = FFT_B
    angb = 2.0 * np.pi * np.outer(np.arange(b), np.arange(b)) / b
    cb, sb = np.cos(angb), np.sin(angb)
    m_fwd = np.block([[cb, sb], [-sb, cb]])
    m_inv = np.block([[cb, -sb], [sb, cb]])
    as_bf16 = lambda m: jnp.asarray(m, dtype=F32).astype(BF16)
    return as_bf16(m_fwd), as_bf16(m_inv)


def _stage1_matrices(l):
    b = FFT_B
    a = l // b
    na = 2 * a
    n = 2 * l
    idx = jnp.arange(na, dtype=jnp.int32)
    alpha = ((idx[:, None] * idx[None, :]) % na).astype(F32) * (2.0 * math.pi / na)
    beta = ((jnp.arange(b, dtype=jnp.int32)[:, None] * idx[None, :]) % n).astype(F32) * (2.0 * math.pi / n)
    ca, sa, cb, sb = jnp.cos(alpha), jnp.sin(alpha), jnp.cos(beta), jnp.sin(beta)
    cb1, sb1 = cb[:, :, None], sb[:, :, None]
    g1 = jnp.concatenate([ca[None] * cb1 - sa[None] * sb1, -(sa[None] * cb1 + ca[None] * sb1)], axis=1).astype(BF16)
    ca3, sa3 = ca[None, :a, :] * (1.0 / n), sa[None, :a, :] * (1.0 / n)
    cb3, sb3 = cb[:, None, :], sb[:, None, :]
    g3 = jnp.concatenate([ca3 * cb3 - sa3 * sb3, -(sa3 * cb3 + ca3 * sb3)], axis=2).astype(BF16)
    return g1, g3


def _cmul(ar, ai, br, bi):
    return ar * br - ai * bi, ar * bi + ai * br


def _pack_complex(re, im):
    r = lax.bitcast_convert_type(re.astype(BF16).astype(F32), U32)
    i = lax.bitcast_convert_type(im.astype(BF16).astype(F32), U32)
    return r | (i >> 16)


def _unpack_complex(w):
    re = lax.bitcast_convert_type(w & jnp.uint32(0xFFFF0000), F32)
    im = lax.bitcast_convert_type(w << 16, F32)
    return re, im


def _stacked_bf16(w):
    re, im = _unpack_complex(w)
    return jnp.concatenate([re, im], axis=0).astype(BF16)


def _strided_rows(chunk_refs, s, rows, stride):
    flat = [r.reshape(rows * stride, LANES) for r in chunk_refs]
    return jnp.concatenate([r[pl.ds(s, rows, stride=stride), :] for r in flat], axis=-1)


def _pack_stacked(y):
    half = y.shape[0] // 2
    return _pack_complex(y[:half], y[half:])


def _fft_stage1_kernel(*refs, segs, nq, strided):
    f_ref, x_refs, o_ref = refs[0], refs[1:1 + nq], refs[1 + nq]
    ka = f_ref.shape[2]
    for s in range(segs):
        x = _strided_rows(x_refs, s, ka, segs) if strided else x_refs[0][s]
        o_ref[s] = _pack_stacked(_dot(f_ref[s], x.astype(BF16)))


def _fft_stage1(x4, part, g1, strided, cb=512):
    if strided:
        _, ka, b, c = x4.shape
    else:
        _, b, ka, c = x4.shape
    rows = g1.shape[1]
    segs = SUBLANES
    cb = min(cb, c)
    if strided:
        nq = cb // LANES
        x_specs = [pl.BlockSpec((None, ka, segs, LANES), functools.partial(lambda j, jc, q: (part, 0, j, jc * nq + q), q=q))
                   for q in range(nq)]
    else:
        nq = 1
        x_specs = [pl.BlockSpec((None, segs, ka, cb), lambda j, jc: (part, j, 0, jc))]
    return pl.pallas_call(
        functools.partial(_fft_stage1_kernel, segs=segs, nq=nq, strided=strided),
        grid=(b // segs, c // cb),
        in_specs=[pl.BlockSpec((segs, rows, ka), lambda j, jc: (j, 0, 0))] + x_specs,
        out_specs=pl.BlockSpec((segs, rows // 2, cb), lambda j, jc: (j, 0, jc)),
        out_shape=_sds((b, rows // 2, c), U32),
        compiler_params=_params("parallel", "parallel"),
        name="fft_stage1",
    )(g1, *([x4] * nq))


def _fft_filter_stage1_kernel(z_ref, w1_ref, b1_ref, w2_ref, b2_ref, w3_ref, b3_ref, fr_ref, w4_ref, dl_ref, f_ref,
                              o_ref, h_ref, *, seq, segs):
    a_n = z_ref.shape[1]
    inv = 1.0 / (seq - 1)

    @pl.when(pl.program_id(1) == 0)
    def _():
        for s in range(segs):
            h = jnp.sin(fr_ref[0:1, :] * (_dot(z_ref[s], w1_ref[...]) + b1_ref[...]))
            h = jnp.sin(fr_ref[1:2, :] * (_dot(h.astype(BF16), w2_ref[...]) + b2_ref[...]))
            h = jnp.sin(fr_ref[2:3, :] * (_dot(h.astype(BF16), w3_ref[...]) + b3_ref[...]))
            h_ref[s] = h.astype(BF16)

    for s in range(segs):
        b = pl.program_id(0) * segs + s
        n = (lax.broadcasted_iota(jnp.int32, (a_n, 1), 0) * FFT_B + b).astype(F32)
        e_fwd = jnp.exp(-(n * inv) * dl_ref[...])
        e_rev = jnp.where(n == 0.0, 0.0, jnp.exp(-((seq - n) * inv) * dl_ref[...]))
        h = h_ref[s]
        for o in range(HY_ORDER):
            k = jnp.concatenate([_dot(h, w4_ref[2 * o]) * e_fwd, _dot(h, w4_ref[2 * o + 1]) * e_rev], axis=0)
            o_ref[o, s] = _pack_stacked(_dot(f_ref[s], k.astype(BF16)))


def _block_diag2(w):
    z = jnp.zeros_like(w)
    return jnp.concatenate([jnp.concatenate([w, z], axis=1), jnp.concatenate([z, w], axis=1)], axis=0)


def _fft_filter_stage1(l, c, w1, b1, w2, b2, w3, b3, freq, w4, g1, cb=256):
    fw = w1.shape[1]
    emb = 64
    a = l // FFT_B
    rows = g1.shape[1]
    segs = SUBLANES
    cb = min(cb, c)
    z = _hyena_pos_features(l)
    z_rev = jnp.concatenate([z[:1], z[:0:-1]], axis=0)
    pad = lambda t: jnp.pad(t, ((0, 0), (0, emb - HY_EMB)))
    z2 = jnp.concatenate([pad(z), pad(z_rev)], axis=1).astype(BF16)
    z2 = z2.reshape(a, FFT_B, 2 * emb).transpose(1, 0, 2)
    both = lambda v: jnp.concatenate([v, v]).reshape(1, 2 * fw)
    w1d = _block_diag2(jnp.pad(w1, ((0, emb - HY_EMB), (0, 0)))).astype(BF16)
    w4r = w4.reshape(fw, HY_ORDER * 2, c).transpose(1, 0, 2)
    zeros = jnp.zeros_like(w4r)
    fwd_dir = (jnp.arange(HY_ORDER * 2) % 2 == 0)[:, None, None]
    w4d = jnp.concatenate([jnp.where(fwd_dir, w4r, zeros), jnp.where(fwd_dir, zeros, w4r)], axis=1).astype(BF16)
    max_decay = math.log(HY_DECAY_TARGET) / HY_FAST_PCT
    min_decay = math.log(HY_DECAY_TARGET) / HY_SLOW_PCT
    deltas = jnp.abs(jnp.linspace(min_decay, max_decay, c, dtype=F32)).reshape(1, c)
    full = lambda shape: pl.BlockSpec(shape, lambda j, jc: (0,) * len(shape))
    return pl.pallas_call(
        functools.partial(_fft_filter_stage1_kernel, seq=l, segs=segs),
        grid=(FFT_B // segs, c // cb),
        in_specs=[
            pl.BlockSpec((segs, a, 2 * emb), lambda j, jc: (j, 0, 0)),
            full((2 * emb, 2 * fw)), full((1, 2 * fw)), full((2 * fw, 2 * fw)), full((1, 2 * fw)),
            full((2 * fw, 2 * fw)), full((1, 2 * fw)), full((3, 2 * fw)),
            pl.BlockSpec((HY_ORDER * 2, 2 * fw, cb), lambda j, jc: (0, 0, jc)),
            pl.BlockSpec((1, cb), lambda j, jc: (0, jc)),
            pl.BlockSpec((segs, rows, 2 * a), lambda j, jc: (j, 0, 0)),
        ],
        out_specs=pl.BlockSpec((HY_ORDER, segs, rows // 2, cb), lambda j, jc: (0, j, 0, jc)),
        out_shape=_sds((HY_ORDER, FFT_B, rows // 2, c), U32),
        scratch_shapes=[pltpu.VMEM((segs, a, 2 * fw), BF16)],
        compiler_params=_params("parallel", "arbitrary"),
        name="fft_filter_stage1",
    )(z2, w1d, both(b1), _block_diag2(w2).astype(BF16), both(b2), _block_diag2(w3).astype(BF16), both(b3),
      jnp.concatenate([freq, freq], axis=1), w4d, deltas, g1)


def _fft_mid_kernel(*refs, kq, nc):
    mf_ref, mi_ref = refs[0], refs[1]
    y_refs, k_refs = refs[2:2 + nc], refs[2 + nc:2 + 2 * nc]
    o_ref = refs[2 + 2 * nc]
    b = mf_ref.shape[0] // 2
    for q in range(kq):
        x = _dot(mf_ref[...], _stacked_bf16(_strided_rows(y_refs, q, b, kq)))
        k = _dot(mf_ref[...], _stacked_bf16(_strided_rows(k_refs, q, b, kq)))
        pr, pi = _cmul(x[:b], x[b:], k[:b], k[b:])
        o_ref[q] = _pack_stacked(_dot(mi_ref[...], jnp.concatenate([pr, pi], axis=0).astype(BF16)))


def _fft_mid(y3, yk4, order, m_fwd, m_inv):
    b, na, c = y3.shape
    kq = SUBLANES
    nc = c // LANES
    mat = pl.BlockSpec((2 * b, 2 * b), lambda i: (0, 0))
    y_specs = [pl.BlockSpec((b, kq, LANES), functools.partial(lambda i, q: (0, i, q), q=q)) for q in range(nc)]
    k_specs = [pl.BlockSpec((None, b, kq, LANES), functools.partial(lambda i, q: (order, 0, i, q), q=q))
               for q in range(nc)]
    return pl.pallas_call(
        functools.partial(_fft_mid_kernel, kq=kq, nc=nc),
        grid=(na // kq,),
        in_specs=[mat, mat] + y_specs + k_specs,
        out_specs=pl.BlockSpec((kq, b, c), lambda i: (i, 0, 0)),
        out_shape=_sds((na, b, c), U32),
        compiler_params=_params("parallel"),
        name="fft_mid",
    )(m_fwd, m_inv, *([y3] * nc), *([yk4] * nc))


def _fft_stage3_kernel(*refs, segs, nq, u_strided, head_dim):
    f_ref, v_refs = refs[0], refs[1:1 + nq]
    nu = nq if u_strided else 1
    u_refs, g_refs = refs[1 + nq:1 + nq + nu], refs[1 + nq + nu:1 + 2 * nq + nu]
    sk_ref, ng_ref, o_ref = refs[1 + 2 * nq + nu:]
    a = f_ref.shape[1]
    rows = f_ref.shape[2] // 2
    c = o_ref.shape[-1]
    for s in range(segs):
        y = _dot(f_ref[s], _stacked_bf16(_strided_rows(v_refs, s, rows, segs)))
        u = _strided_rows(u_refs, s, a, segs) if u_strided else u_refs[0][s]
        z = _strided_rows(g_refs, s, a, segs) * (y + u * sk_ref[...])
        if head_dim:
            heads = [z[:, h:h + head_dim] for h in range(0, c, head_dim)]
            heads = [zh * lax.rsqrt(jnp.mean(zh * zh, axis=-1, keepdims=True) + EPS) for zh in heads]
            z = jnp.concatenate(heads, axis=-1) * ng_ref[...]
        o_ref[s] = z


def _fft_stage3(v3, f3, u4, u_part, u_strided, g4, g_part, skip, norm_g, head_dim, cb=512):
    a = f3.shape[1]
    rows, b, c = v3.shape
    segs = SUBLANES
    cb = min(cb, c)
    nq = cb // LANES
    chunk = lambda shape, idx: [pl.BlockSpec(shape, functools.partial(idx, q=q)) for q in range(nq)]
    v_specs = chunk((rows, segs, LANES), lambda j, jc, q: (0, j, jc * nq + q))
    nat = lambda part: chunk((None, a, segs, LANES), lambda j, jc, q: (part, 0, j, jc * nq + q))
    u_specs = nat(u_part) if u_strided else [pl.BlockSpec((None, segs, a, cb), lambda j, jc: (u_part, j, 0, jc))]
    vec = pl.BlockSpec((1, cb), lambda j, jc: (0, jc))
    return pl.pallas_call(
        functools.partial(_fft_stage3_kernel, segs=segs, nq=nq, u_strided=u_strided, head_dim=head_dim),
        grid=(b // segs, c // cb),
        in_specs=[pl.BlockSpec((segs, a, 2 * rows), lambda j, jc: (j, 0, 0))] + v_specs + u_specs + nat(g_part)
        + [vec, vec],
        out_specs=pl.BlockSpec((segs, a, cb), lambda j, jc: (j, 0, jc)),
        out_shape=_sds((b, a, c), F32),
        compiler_params=_params("parallel", "parallel"),
        name="fft_stage3",
    )(f3, *([v3] * nq), *([u4] * (nq if u_strided else 1)), *([g4] * nq), skip.reshape(1, c), norm_g.reshape(1, c))


def _hyena_mixer(proj, conv_w, conv_b, filt, skip, norm_g, c):
    l = proj.shape[0]
    a = l // FFT_B
    hyc4 = _dwconv1d(proj, conv_w, conv_b, 3, c).reshape(3, a, FFT_B, c)
    m_fwd, m_inv = _stage2_matrices()
    g1, f3 = _stage1_matrices(l)
    yk = _fft_filter_stage1(l, c, *filt, g1)
    z4, part, strided = hyc4, 0, True
    for o in range(HY_ORDER):
        y3 = _fft_stage1(z4, part, g1, strided)
        v3 = _fft_mid(y3, yk, o, m_fwd, m_inv)
        last = o == HY_ORDER - 1
        z = _fft_stage3(v3, f3, z4, part, strided, hyc4, 1 + o, skip[o], norm_g, c // HY_HEADS if last else 0)
        z4, part, strided = z[None], 0, False
    return z


def _lower_bound(lg_ref, slot):
    lg = lg_ref[...]
    e = jnp.exp(lg - jnp.max(lg, axis=0, keepdims=True))
    sm = e / jnp.sum(e, axis=0, keepdims=True)
    return jnp.sum(sm[:slot + 1], axis=0, keepdims=True)


def _prefix_sum_rows(x):
    t = x.shape[0]
    row = lax.broadcasted_iota(jnp.int32, x.shape, 0)
    s = 1
    while s < t:
        x = x + jnp.where(row >= s, pltpu.roll(x, s, 0), 0.0)
        s *= 2
    return x


def _gla_kernel(*refs, slot, reverse, nchunk, chunk, hpb, with_out, combine):
    it = iter(refs)
    i_ref, f_ref, lg_ref, s0_ref = next(it), next(it), next(it), next(it)
    q_ref = next(it) if with_out else None
    ofw_ref, g_ref, ng_ref = (next(it), next(it), next(it)) if combine else (None, None, None)
    o_ref = next(it) if with_out else None
    sf_ref, st_ref = next(it), next(it)
    ke = HG_EXPAND

    @pl.when(pl.program_id(1) == 0)
    def _():
        st_ref[...] = s0_ref[...]

    lb = _lower_bound(lg_ref, slot)
    row = lax.broadcasted_iota(jnp.int32, (chunk, chunk), 0)
    col = lax.broadcasted_iota(jnp.int32, (chunk, chunk), 1)
    keep = (col >= row) if reverse else (col <= row)
    nt = (((1,), (1,)), ((), ()))
    tn = (((0,), (0,)), ((), ()))
    st = [st_ref[h] for h in range(hpb)]
    order = range(nchunk - 1, -1, -1) if reverse else range(nchunk)
    for ci in order:
        rows = slice(ci * chunk, (ci + 1) * chunk)
        f = lb + (1.0 - lb) * jax.nn.sigmoid(f_ref[rows, :])
        logf = jnp.log(f)
        key = 1.0 - f
        b = _prefix_sum_rows(logf)
        total = b[chunk - 1:chunk, :]
        if reverse:
            b = total - b + logf
        v = i_ref[rows, :].astype(BF16)
        k_state = (key * jnp.exp(total - b)).astype(BF16)
        decay = jnp.exp(total)
        if with_out:
            q = q_ref[rows, :]
            q_in = (q * jax.nn.sigmoid(q) * jnp.exp(b)).astype(BF16)
            k_intra = (key * jnp.exp(-b)).astype(BF16)
        outs = []
        for h in range(hpb):
            hl = slice(h * ke, (h + 1) * ke)
            if with_out:
                att = lax.dot_general(q_in[:, hl], k_intra[:, hl], nt, preferred_element_type=F32)
                att = jnp.where(keep, att, 0.0).astype(BF16)
                o = _dot(att, v[:, hl]) + lax.dot_general(q_in[:, hl], st[h].astype(BF16), nt,
                                                           preferred_element_type=F32)
                if combine:
                    o = o + ofw_ref[rows, hl]
                    o = o * lax.rsqrt(jnp.mean(o * o, axis=-1, keepdims=True) + EPS)
                outs.append(o)
            st[h] = st[h] * decay[:, hl] + lax.dot_general(v[:, hl], k_state[:, hl], tn, preferred_element_type=F32)
        if with_out:
            o = jnp.concatenate(outs, axis=-1)
            if combine:
                g = g_ref[rows, :]
                o = o * ng_ref[...] * (g * jax.nn.sigmoid(g))
            o_ref[rows, :] = o.astype(o_ref.dtype)
    for h in range(hpb):
        st_ref[h] = st[h]

    @pl.when(pl.program_id(1) == pl.num_programs(1) - 1)
    def _():
        for h in range(hpb):
            sf_ref[h] = st[h]


def _gla(src, cols, lb_logits, slot, s0, reverse, with_out, fwd_out=None, norm_g=None, out_dtype=F32, tb=512, hpb=4):
    l = src.shape[0]
    heads = s0.shape[0]
    ke = HG_EXPAND
    tb = min(tb, l)
    nblk = l // tb
    while heads % hpb or any(off % hpb for off in cols.values()):
        hpb //= 2
    wd = hpb * ke
    combine = fwd_out is not None
    blk = (lambda j: nblk - 1 - j) if reverse else (lambda j: j)
    col = lambda name: pl.BlockSpec((tb, wd), lambda h, j: (blk(j), cols[name] // hpb + h))
    head_rows = pl.BlockSpec((tb, wd), lambda h, j: (blk(j), h))
    state = pl.BlockSpec((hpb, ke, ke), lambda h, j: (h, 0, 0))
    in_specs = [col("i"), col("f"), pl.BlockSpec((lb_logits.shape[0], wd), lambda h, j: (0, h)), state]
    args = [src, src, lb_logits, s0]
    out_specs, out_shape = [], []
    if with_out:
        in_specs.append(col("q"))
        args.append(src)
        out_specs.append(head_rows)
        out_shape.append(_sds((l, heads * ke), out_dtype))
    if combine:
        in_specs += [head_rows, col("g"), pl.BlockSpec((1, wd), lambda h, j: (0, h))]
        args += [fwd_out, src, norm_g.reshape(1, heads * ke)]
    out_specs.append(state)
    out_shape.append(_sds((heads, ke, ke), F32))
    res = pl.pallas_call(
        functools.partial(_gla_kernel, slot=slot, reverse=reverse, nchunk=tb // HG_CHUNK, chunk=HG_CHUNK, hpb=hpb,
                          with_out=with_out, combine=combine),
        grid=(heads // hpb, nblk),
        in_specs=in_specs,
        out_specs=out_specs,
        out_shape=out_shape,
        scratch_shapes=[pltpu.VMEM((hpb, ke, ke), F32)],
        compiler_params=_params("parallel", "arbitrary"),
        name="hgrn_gla",
    )(*args)
    return res if with_out else (None, res[0])


def _hgrn_mixer(proj, q0, ctx_proj, lb_logits, slot, norm_g, heads):
    qb = q0 // HG_EXPAND
    zeros = jnp.zeros((heads, HG_EXPAND, HG_EXPAND), F32)
    _, s0_fw = _gla(ctx_proj, {"i": 0, "f": heads}, lb_logits, slot, zeros, False, False)
    _, s0_bw = _gla(ctx_proj, {"i": 0, "f": 2 * heads}, lb_logits, slot, zeros, True, False)
    cols = {"q": qb, "i": qb + heads, "f": qb + 2 * heads, "g": qb + 4 * heads}
    o_fw, _ = _gla(proj, cols, lb_logits, slot, s0_fw, False, True)
    cols["f"] = qb + 3 * heads
    y, _ = _gla(proj, cols, lb_logits, slot, s0_bw, True, True, fwd_out=o_fw, norm_g=norm_g, out_dtype=BF16)
    return y


def _outproj_kernel(*refs, na, nc):
    a_refs = refs[:nc]
    b_ref, w_ref, x_ref, g_ref, o_ref = refs[nc:]
    rows = a_refs[0].shape[0]
    k1 = nc * LANES
    ya = jnp.concatenate([_strided_rows(a_refs, s, rows, na) for s in range(na)], axis=0).astype(BF16)
    y = _dot(ya, w_ref[:k1, :]) + _dot(b_ref[...], w_ref[k1:, :])
    o_ref[...] = x_ref[...] + g_ref[...] * y


def _outproj(ya3, yb, w, x, gate, tn=1024):
    m, d = x.shape
    b, a, k1 = ya3.shape
    k2 = yb.shape[1]
    na = min(SUBLANES, a)
    tm = na * b
    tn = min(tn, d)
    nc = k1 // LANES
    a_specs = [pl.BlockSpec((b, na, LANES), functools.partial(lambda j, i, q: (0, i, q), q=q)) for q in range(nc)]
    return pl.pallas_call(
        functools.partial(_outproj_kernel, na=na, nc=nc),
        grid=(d // tn, m // tm),
        in_specs=a_specs + [
            pl.BlockSpec((tm, k2), lambda j, i: (i, 0)),
            pl.BlockSpec((k1 + k2, tn), lambda j, i: (0, j)),
            pl.BlockSpec((tm, tn), lambda j, i: (i, j)),
            pl.BlockSpec((1, tn), lambda j, i: (0, j)),
        ],
        out_specs=pl.BlockSpec((tm, tn), lambda j, i: (i, j)),
        out_shape=_sds((m, d), F32),
        compiler_params=_params("parallel", "parallel"),
        name="outproj_residual",
    )(*([ya3] * nc), yb, w, x, gate)


def _ffn_kernel(h_hbm, wa_ref, wu_ref, cw_ref, cb_ref, wd_ref, x_hbm, g_ref, fg_ref, o_ref,
                hext_ref, x_ref, act_ref, sem, *, gw, final):
    i, j = pl.program_id(0), pl.program_id(1)
    ni, nj = pl.num_programs(0), pl.num_programs(1) - 1
    tm = o_ref.shape[0]
    n = tm + 2 * gw
    row0 = i * tm

    def h_copy(src_row, dst_row, rows, k):
        return pltpu.make_async_copy(h_hbm.at[pl.ds(src_row, rows)], hext_ref.at[pl.ds(dst_row, rows)], sem.at[k])

    x_copy = pltpu.make_async_copy(x_hbm.at[pl.ds(row0, tm)], x_ref, sem.at[3])
    has_prev, has_next = i > 0, i < ni - 1

    def up_part():
        a = _dot(hext_ref[...], wa_ref[...])
        u = _dot(hext_ref[gw:gw + tm, :], wu_ref[...])
        col = lax.broadcasted_iota(jnp.int32, a.shape, 0) % gw
        a_m1 = jnp.where(col == 0, 0.0, pltpu.roll(a, 1, 0))
        a_p1 = jnp.where(col == gw - 1, 0.0, pltpu.roll(a, n - 1, 0))
        conv = cb_ref[...]
        for di in range(3):
            rows = slice(di * gw, di * gw + tm)
            conv = conv + (a_m1[rows] * cw_ref[3 * di:3 * di + 1, :] + a[rows] * cw_ref[3 * di + 1:3 * di + 2, :]
                           + a_p1[rows] * cw_ref[3 * di + 2:3 * di + 3, :])
        gelu = 0.5 * conv * (1.0 + lax.erf(conv * (1.0 / math.sqrt(2.0))))
        act_ref[j % 2] = (gelu * u).astype(BF16)

    def down_part():
        o_ref[...] += _dot(act_ref[(j + 1) % 2], wd_ref[...])

    @pl.when(j == 0)
    def _():
        h_copy(row0, gw, tm, 0).start()
        x_copy.start()

        @pl.when(has_prev)
        def _():
            h_copy(row0 - gw, 0, gw, 1).start()

        @pl.when(has_next)
        def _():
            h_copy(row0 + tm, gw + tm, gw, 2).start()

        @pl.when(jnp.logical_not(has_prev))
        def _():
            hext_ref[0:gw, :] = jnp.zeros((gw, hext_ref.shape[1]), hext_ref.dtype)

        @pl.when(jnp.logical_not(has_next))
        def _():
            hext_ref[gw + tm:n, :] = jnp.zeros((gw, hext_ref.shape[1]), hext_ref.dtype)

        o_ref[...] = jnp.zeros_like(o_ref)
        h_copy(row0, gw, tm, 0).wait()

        @pl.when(has_prev)
        def _():
            h_copy(row0 - gw, 0, gw, 1).wait()

        @pl.when(has_next)
        def _():
            h_copy(row0 + tm, gw + tm, gw, 2).wait()

        up_part()

    @pl.when(jnp.logical_and(j > 0, j < nj))
    def _():
        down_part()
        up_part()

    @pl.when(j == nj)
    def _():
        down_part()
        x_copy.wait()
        y = x_ref[...] + g_ref[...] * o_ref[...]
        if final:
            y = y * lax.rsqrt(jnp.mean(y * y, axis=-1, keepdims=True) + EPS) * fg_ref[...]
        o_ref[...] = y


def _conv_ffn(h, x, gate, w_up, w_down, layer, conv_w, conv_b, final_g, gw, tf=512):
    l, d = x.shape
    dff = w_down.shape[1]
    tm, tf = min(FFN_ROWS, l), min(tf, dff)
    nj = dff // tf
    final = final_g is not None
    fg = (final_g if final else jnp.ones((d,), F32)).reshape(1, d)
    hbm = pl.BlockSpec(memory_space=pl.ANY)
    up = lambda j: jnp.minimum(j, nj - 1)
    down = lambda j: jnp.maximum(j - 1, 0)
    return pl.pallas_call(
        functools.partial(_ffn_kernel, gw=gw, final=final),
        grid=(l // tm, nj + 1),
        in_specs=[
            hbm,
            pl.BlockSpec((None, d, tf), lambda i, j: (layer, 0, up(j))),
            pl.BlockSpec((None, d, tf), lambda i, j: (layer, 0, nj + up(j))),
            pl.BlockSpec((9, tf), lambda i, j: (0, up(j))),
            pl.BlockSpec((1, tf), lambda i, j: (0, up(j))),
            pl.BlockSpec((None, tf, d), lambda i, j: (layer, down(j), 0)),
            hbm,
            pl.BlockSpec((1, d), lambda i, j: (0, 0)),
            pl.BlockSpec((1, d), lambda i, j: (0, 0)),
        ],
        out_specs=pl.BlockSpec((tm, d), lambda i, j: (i, 0)),
        out_shape=_sds((l, d), F32),
        scratch_shapes=[pltpu.VMEM((tm + 2 * gw, d), BF16), pltpu.VMEM((tm, d), F32),
                        pltpu.VMEM((2, tm, tf), BF16), pltpu.SemaphoreType.DMA((4,))],
        compiler_params=_params("arbitrary", "arbitrary"),
        name="conv_ffn",
    )(h, w_up, w_up, conv_w.reshape(9, dff), conv_b.reshape(1, dff), w_down, x, gate, fg)


def _pool_kernel(xm_ref, xp_ref, xn_ref, ng_ref, sc_ref, sh_ref, w_ref, b_ref, ps_ref, g_ref, o_ref, *, seq,
                 windows):
    i = pl.program_id(0)
    tm = xm_ref.shape[0]
    pd = w_ref.shape[1]
    norm = lambda x: _rms_mod(x, ng_ref[...], sc_ref[...], sh_ref[...])
    xm = xm_ref[...]
    hm = norm(xm)
    hp = jnp.where(i > 0, norm(xp_ref[...]), 0.0)
    hn = jnp.where(i < pl.num_programs(0) - 1, norm(xn_ref[...]), 0.0)
    t = i * tm + lax.broadcasted_iota(jnp.int32, (tm, 1), 0)
    n = tm + 2 * POOL_HALO
    for gi, win in enumerate(windows):
        cols = slice(gi * pd, (gi + 1) * pd)
        e = jnp.concatenate([hp[:, cols], hm[:, cols], hn[:, cols]], axis=0)
        p = e + pltpu.roll(e, 1, 0)
        w = 2
        while w < win:
            p = pltpu.roll(p, w // 2, 0) + pltpu.roll(p, n - w // 2, 0)
            w *= 2
        count = (jnp.clip(t + win // 2, 0, seq) - jnp.clip(t - win // 2, 0, seq)).astype(F32)
        mean = p[POOL_HALO:POOL_HALO + tm] / count
        y = _dot((mean - hm[:, cols]).astype(BF16), w_ref[gi]) + b_ref[gi]
        o_ref[:, cols] = xm[:, cols] + g_ref[:, cols] * (y * ps_ref[:, cols])


def _pool_layer(x, norm_g, sc, sh, w, b, scale, gate, tm=512):
    l, d = x.shape
    ng, pd, _ = w.shape
    assert max(POOL_WINDOWS) // 2 <= POOL_HALO == SUBLANES
    tm = min(tm, l)
    r = tm // SUBLANES
    last = l // SUBLANES - 1
    row = pl.BlockSpec((1, d), lambda i: (0, 0))
    return pl.pallas_call(
        functools.partial(_pool_kernel, seq=l, windows=POOL_WINDOWS),
        grid=(l // tm,),
        in_specs=[
            pl.BlockSpec((tm, d), lambda i: (i, 0)),
            pl.BlockSpec((SUBLANES, d), lambda i: (jnp.maximum(i * r - 1, 0), 0)),
            pl.BlockSpec((SUBLANES, d), lambda i: (jnp.minimum((i + 1) * r, last), 0)),
            row, row, row,
            pl.BlockSpec((ng, pd, pd), lambda i: (0, 0, 0)),
            pl.BlockSpec((ng, 1, pd), lambda i: (0, 0, 0)),
            row, row,
        ],
        out_specs=pl.BlockSpec((tm, d), lambda i: (i, 0)),
        out_shape=_sds((l, d), F32),
        compiler_params=_params("parallel"),
        name="pool_mixer_residual",
    )(x, x, x, norm_g.reshape(1, d), sc, sh, w.astype(BF16), b.reshape(ng, 1, pd), scale.reshape(1, d), gate)


def kernel(x, c, ctx, c_ctx, norm_mix_g, norm_ffn_g, mod_w, mod_b, in_w, in_b, hy_conv_w, hy_conv_b, hy_w1, hy_b1, hy_w2, hy_b2, hy_w3, hy_b3, hy_freq, hy_w4, hy_skip, hy_norm_g, hg_lb_logits, hg_norm_g, out_w, pool_w, pool_b, pool_scale, ffn_up_w, ffn_conv_w, ffn_conv_b, ffn_down_w, final_norm_g):
    batch, _, d = x.shape
    depth = mod_w.shape[0]
    hy_d = hy_norm_g.shape[1]
    hg_d = hg_norm_g.shape[1]
    hg_q0 = 3 * hy_d
    hg_i0 = hg_q0 + hg_d
    hg_heads = hg_d // HG_EXPAND
    ffn_up_bf, ffn_down_bf = ffn_up_w.astype(BF16), ffn_down_w.astype(BF16)
    outs = []
    for bi in range(batch):
        xs = x[bi]
        cvecs = jnp.zeros((SUBLANES, d), F32).at[0].set(c[bi]).at[1].set(c_ctx)
        for l in range(depth):
            mod = _adaln(cvecs, mod_w, mod_b, l)
            sh1, sc1, g1, sh2, sc2, g2 = [mod[0:1, k * d:(k + 1) * d] for k in range(N_MOD)]
            if l % 2 == 0:
                e = l // 2
                in_w_bf = in_w[e].astype(BF16)
                proj = _norm_matmul(xs, norm_mix_g[l], sc1, sh1, in_w_bf, in_b[e])
                ctx_proj = _norm_matmul(ctx[bi], norm_mix_g[l], mod[1:2, d:2 * d], mod[1:2, 0:d],
                                        in_w_bf[:, hg_i0:hg_i0 + 3 * hg_d], in_b[e, hg_i0:hg_i0 + 3 * hg_d])
                filt = (hy_w1[e], hy_b1[e], hy_w2[e], hy_b2[e], hy_w3[e], hy_b3[e], hy_freq[e], hy_w4[e])
                y_hy = _hyena_mixer(proj, hy_conv_w[e], hy_conv_b[e], filt, hy_skip[e], hy_norm_g[e], hy_d)
                y_hg = _hgrn_mixer(proj, hg_q0, ctx_proj, hg_lb_logits, e, hg_norm_g[e], hg_heads)
                xs = _outproj(y_hy, y_hg, out_w[e].astype(BF16), xs, g1)
            else:
                od = l // 2
                xs = _pool_layer(xs, norm_mix_g[l], sc1, sh1, pool_w[od], pool_b[od], pool_scale[od], g1)
            h = _norm_mod(xs, norm_ffn_g[l], sc2, sh2)
            xs = _conv_ffn(h, xs, g2, ffn_up_bf, ffn_down_bf, l, ffn_conv_w[l], ffn_conv_b[l],
                           final_norm_g if l == depth - 1 else None, GRID_W)
        outs.append(xs[None])
    return outs[0] if batch == 1 else jnp.concatenate(outs, axis=0)
```

```python
import functools
import math

import numpy as np
import jax
import jax.numpy as jnp
from jax import lax
from jax.experimental import pallas as pl
from jax.experimental.pallas import tpu as pltpu

F32 = jnp.float32
BF16 = jnp.bfloat16
U32 = jnp.uint32
EPS = 1e-6

LANES = 128
SUBLANES = 8
VMEM_LIMIT_BYTES = 56 * 1024 * 1024

GRID_W = 64
N_MOD = 6
HY_HEADS = 8
HY_ORDER = 2
HY_EMB = 33
HY_DECAY_TARGET = 1e-2
HY_FAST_PCT = 0.3
HY_SLOW_PCT = 1.5
HG_EXPAND = 128
HG_CHUNK = 64
POOL_WINDOWS = (2, 4, 8, 16)
POOL_HALO = 8
FFT_B = 128
FFN_ROWS = 1024


def _params(*sem):
    return pltpu.CompilerParams(dimension_semantics=sem, vmem_limit_bytes=VMEM_LIMIT_BYTES)


def _sds(shape, dtype):
    return jax.ShapeDtypeStruct(shape, dtype)


def _dot(a, b):
    return jnp.dot(a, b, preferred_element_type=F32)


def _adaln_kernel(c_ref, w_ref, b_ref, o_ref):
    c = c_ref[...]
    s = (c * jax.nn.sigmoid(c)).astype(BF16)
    o_ref[...] = _dot(s, w_ref[...].astype(BF16)) + b_ref[...]


def _adaln(cvecs, mod_w, mod_b, layer, tn=1024):
    _, d, n = mod_w.shape
    tn = min(tn, n)
    return pl.pallas_call(
        _adaln_kernel,
        grid=(n // tn,),
        in_specs=[
            pl.BlockSpec((SUBLANES, d), lambda j: (0, 0)),
            pl.BlockSpec((None, d, tn), lambda j: (layer, 0, j)),
            pl.BlockSpec((None, 1, tn), lambda j: (layer, 0, j)),
        ],
        out_specs=pl.BlockSpec((SUBLANES, tn), lambda j: (0, j)),
        out_shape=_sds((SUBLANES, n), F32),
        compiler_params=_params("arbitrary"),
        name="adaln",
    )(cvecs, mod_w, mod_b.reshape(mod_b.shape[0], 1, n))


def _rms_mod(x, g, sc, sh):
    y = x * lax.rsqrt(jnp.mean(x * x, axis=-1, keepdims=True) + EPS) * g
    return y * (1.0 + sc) + sh


def _norm_mod_kernel(x_ref, g_ref, sc_ref, sh_ref, o_ref):
    o_ref[...] = _rms_mod(x_ref[...], g_ref[...], sc_ref[...], sh_ref[...]).astype(o_ref.dtype)


def _norm_mod(x, g, sc, sh, tm=1024):
    m, d = x.shape
    tm = min(tm, m)
    row = pl.BlockSpec((1, d), lambda i: (0, 0))
    return pl.pallas_call(
        _norm_mod_kernel,
        grid=(m // tm,),
        in_specs=[pl.BlockSpec((tm, d), lambda i: (i, 0)), row, row, row],
        out_specs=pl.BlockSpec((tm, d), lambda i: (i, 0)),
        out_shape=_sds((m, d), BF16),
        compiler_params=_params("parallel"),
        name="norm_mod",
    )(x, g.reshape(1, d), sc, sh)


def _norm_matmul_kernel(x_ref, g_ref, sc_ref, sh_ref, w_ref, b_ref, o_ref, h_ref):
    @pl.when(pl.program_id(1) == 0)
    def _():
        h_ref[...] = _rms_mod(x_ref[...], g_ref[...], sc_ref[...], sh_ref[...]).astype(h_ref.dtype)

    o_ref[...] = _dot(h_ref[...], w_ref[...]) + b_ref[...]


def _norm_matmul(x, g, sc, sh, w, b, tm=1024, tn=1024):
    m, k = x.shape
    n = w.shape[1]
    tm, tn = min(tm, m), min(tn, n)
    row = pl.BlockSpec((1, k), lambda i, j: (0, 0))
    return pl.pallas_call(
        _norm_matmul_kernel,
        grid=(m // tm, n // tn),
        in_specs=[
            pl.BlockSpec((tm, k), lambda i, j: (i, 0)), row, row, row,
            pl.BlockSpec((k, tn), lambda i, j: (0, j)),
            pl.BlockSpec((1, tn), lambda i, j: (0, j)),
        ],
        out_specs=pl.BlockSpec((tm, tn), lambda i, j: (i, j)),
        out_shape=_sds((m, n), F32),
        scratch_shapes=[pltpu.VMEM((tm, k), BF16)],
        compiler_params=_params("parallel", "arbitrary"),
        name="norm_matmul",
    )(x, g.reshape(1, k), sc, sh, w, b.reshape(1, n))


def _dwconv1d_kernel(xm_ref, xp_ref, xn_ref, w_ref, b_ref, o_ref):
    i = pl.program_id(0)
    x = xm_ref[...]
    tm = x.shape[0]
    prev_row = jnp.where(i > 0, xp_ref[SUBLANES - 1:SUBLANES, :], 0.0)
    next_row = jnp.where(i < pl.num_programs(0) - 1, xn_ref[0:1, :], 0.0)
    row = lax.broadcasted_iota(jnp.int32, x.shape, 0)
    x_m1 = jnp.where(row == 0, prev_row, pltpu.roll(x, 1, 0))
    x_p1 = jnp.where(row == tm - 1, next_row, pltpu.roll(x, tm - 1, 0))
    o_ref[...] = x_m1 * w_ref[0:1, :] + x * w_ref[1:2, :] + x_p1 * w_ref[2:3, :] + b_ref[...]


def _dwconv1d(proj, w, b, parts, c, tm=1024):
    l = proj.shape[0]
    tm = min(tm, l)
    r = tm // SUBLANES
    last = l // SUBLANES - 1
    return pl.pallas_call(
        _dwconv1d_kernel,
        grid=(l // tm, parts),
        in_specs=[
            pl.BlockSpec((tm, c), lambda i, j: (i, j)),
            pl.BlockSpec((SUBLANES, c), lambda i, j: (jnp.maximum(i * r - 1, 0), j)),
            pl.BlockSpec((SUBLANES, c), lambda i, j: (jnp.minimum((i + 1) * r, last), j)),
            pl.BlockSpec((3, c), lambda i, j: (0, j)),
            pl.BlockSpec((1, c), lambda i, j: (0, j)),
        ],
        out_specs=pl.BlockSpec((None, tm, c), lambda i, j: (j, i, 0)),
        out_shape=_sds((parts, l, c), F32),
        compiler_params=_params("parallel", "parallel"),
        name="hyena_dwconv1d",
    )(proj, proj, proj, w, b.reshape(1, parts * c))


def _hyena_pos_features(l):
    t = jnp.linspace(0.0, 1.0, l, dtype=F32)[:, None]
    bands = (HY_EMB - 1) // 2
    f = jnp.linspace(1e-4, bands - 1, bands, dtype=F32)[None, :]
    w = (2.0 * math.pi / l) * jnp.arange(l, dtype=F32)[:, None]
    return jnp.concatenate([t, jnp.cos(f * w), -jnp.sin(f * w)], axis=-1)


def _stage2_matrices():
    b = FFT_B
    angb = 2.0 * np.pi * np.outer(np.arange(b), np.arange(b)) / b
    cb, sb = np.cos(angb), np.sin(angb)
    m_fwd = np.block([[cb, sb], [-sb, cb]])
    m_inv = np.block([[cb, -sb], [sb, cb]])
    as_bf16 = lambda m: jnp.asarray(m, dtype=F32).astype(BF16)
    return as_bf16(m_fwd), as_bf16(m_inv)


def _stage1_matrices(l):
    b = FFT_B
    a = l // b
    na = 2 * a
    n = 2 * l
    idx = jnp.arange(na, dtype=jnp.int32)
    alpha = ((idx[:, None] * idx[None, :]) % na).astype(F32) * (2.0 * math.pi / na)
    beta = ((jnp.arange(b, dtype=jnp.int32)[:, None] * idx[None, :]) % n).astype(F32) * (2.0 * math.pi / n)
    ca, sa, cb, sb = jnp.cos(alpha), jnp.sin(alpha), jnp.cos(beta), jnp.sin(beta)
    cb1, sb1 = cb[:, :, None], sb[:, :, None]
    g1 = jnp.concatenate([ca[None] * cb1 - sa[None] * sb1, -(sa[None] * cb1 + ca[None] * sb1)], axis=1).astype(BF16)
    ca3, sa3 = ca[None, :a, :] * (1.0 / n), sa[None, :a, :] * (1.0 / n)
    cb3, sb3 = cb[:, None, :], sb[:, None, :]
    g3 = jnp.concatenate([ca3 * cb3 - sa3 * sb3, -(sa3 * cb3 + ca3 * sb3)], axis=2).astype(BF16)
    return g1, g3


def _cmul(ar, ai, br, bi):
    return ar * br - ai * bi, ar * bi + ai * br


def _pack_complex(re, im):
    r = lax.bitcast_convert_type(re.astype(BF16).astype(F32), U32)
    i = lax.bitcast_convert_type(im.astype(BF16).astype(F32), U32)
    return r | (i >> 16)


def _unpack_complex(w):
    re = lax.bitcast_convert_type(w & jnp.uint32(0xFFFF0000), F32)
    im = lax.bitcast_convert_type(w << 16, F32)
    return re, im


def _stacked_bf16(w):
    re, im = _unpack_complex(w)
    return jnp.concatenate([re, im], axis=0).astype(BF16)


def _strided_rows(chunk_refs, s, rows, stride):
    flat = [r.reshape(rows * stride, LANES) for r in chunk_refs]
    return jnp.concatenate([r[pl.ds(s, rows, stride=stride), :] for r in flat], axis=-1)


def _pack_stacked(y):
    half = y.shape[0] // 2
    return _pack_complex(y[:half], y[half:])


def _fft_stage1_kernel(*refs, segs, nq, strided):
    f_ref, x_refs, o_ref = refs[0], refs[1:1 + nq], refs[1 + nq]
    ka = f_ref.shape[2]
    for s in range(segs):
        x = _strided_rows(x_refs, s, ka, segs) if strided else x_refs[0][s]
        o_ref[s] = _pack_stacked(_dot(f_ref[s], x.astype(BF16)))


def _fft_stage1(x4, part, g1, strided, cb=512):
    if strided:
        _, ka, b, c = x4.shape
    else:
        _, b, ka, c = x4.shape
    rows = g1.shape[1]
    segs = SUBLANES
    cb = min(cb, c)
    if strided:
        nq = cb // LANES
        x_specs = [pl.BlockSpec((None, ka, segs, LANES), functools.partial(lambda j, jc, q: (part, 0, j, jc * nq + q), q=q))
                   for q in range(nq)]
    else:
        nq = 1
        x_specs = [pl.BlockSpec((None, segs, ka, cb), lambda j, jc: (part, j, 0, jc))]
    return pl.pallas_call(
        functools.partial(_fft_stage1_kernel, segs=segs, nq=nq, strided=strided),
        grid=(b // segs, c // cb),
        in_specs=[pl.BlockSpec((segs, rows, ka), lambda j, jc: (j, 0, 0))] + x_specs,
        out_specs=pl.BlockSpec((segs, rows // 2, cb), lambda j, jc: (j, 0, jc)),
        out_shape=_sds((b, rows // 2, c), U32),
        compiler_params=_params("parallel", "parallel"),
        name="fft_stage1",
    )(g1, *([x4] * nq))


def _fft_filter_stage1_kernel(z_ref, w1_ref, b1_ref, w2_ref, b2_ref, w3_ref, b3_ref, fr_ref, w4_ref, dl_ref, f_ref,
                              o_ref, h_ref, *, seq, segs):
    a_n = z_ref.shape[1]
    inv = 1.0 / (seq - 1)

    @pl.when(pl.program_id(1) == 0)
    def _():
        for s in range(segs):
            h = jnp.sin(fr_ref[0:1, :] * (_dot(z_ref[s], w1_ref[...]) + b1_ref[...]))
            h = jnp.sin(fr_ref[1:2, :] * (_dot(h.astype(BF16), w2_ref[...]) + b2_ref[...]))
            h = jnp.sin(fr_ref[2:3, :] * (_dot(h.astype(BF16), w3_ref[...]) + b3_ref[...]))
            h_ref[s] = h.astype(BF16)

    for s in range(segs):
        b = pl.program_id(0) * segs + s
        n = (lax.broadcasted_iota(jnp.int32, (a_n, 1), 0) * FFT_B + b).astype(F32)
        e_fwd = jnp.exp(-(n * inv) * dl_ref[...])
        e_rev = jnp.where(n == 0.0, 0.0, jnp.exp(-((seq - n) * inv) * dl_ref[...]))
        h = h_ref[s]
        for o in range(HY_ORDER):
            k = jnp.concatenate([_dot(h, w4_ref[2 * o]) * e_fwd, _dot(h, w4_ref[2 * o + 1]) * e_rev], axis=0)
            o_ref[o, s] = _pack_stacked(_dot(f_ref[s], k.astype(BF16)))


def _block_diag2(w):
    z = jnp.zeros_like(w)
    return jnp.concatenate([jnp.concatenate([w, z], axis=1), jnp.concatenate([z, w], axis=1)], axis=0)


def _fft_filter_stage1(l, c, w1, b1, w2, b2, w3, b3, freq, w4, g1, cb=256):
    fw = w1.shape[1]
    emb = 64
    a = l // FFT_B
    rows = g1.shape[1]
    segs = SUBLANES
    cb = min(cb, c)
    z = _hyena_pos_features(l)
    z_rev = jnp.concatenate([z[:1], z[:0:-1]], axis=0)
    pad = lambda t: jnp.pad(t, ((0, 0), (0, emb - HY_EMB)))
    z2 = jnp.concatenate([pad(z), pad(z_rev)], axis=1).astype(BF16)
    z2 = z2.reshape(a, FFT_B, 2 * emb).transpose(1, 0, 2)
    both = lambda v: jnp.concatenate([v, v]).reshape(1, 2 * fw)
    w1d = _block_diag2(jnp.pad(w1, ((0, emb - HY_EMB), (0, 0)))).astype(BF16)
    w4r = w4.reshape(fw, HY_ORDER * 2, c).transpose(1, 0, 2)
    zeros = jnp.zeros_like(w4r)
    fwd_dir = (jnp.arange(HY_ORDER * 2) % 2 == 0)[:, None, None]
    w4d = jnp.concatenate([jnp.where(fwd_dir, w4r, zeros), jnp.where(fwd_dir, zeros, w4r)], axis=1).astype(BF16)
    max_decay = math.log(HY_DECAY_TARGET) / HY_FAST_PCT
    min_decay = math.log(HY_DECAY_TARGET) / HY_SLOW_PCT
    deltas = jnp.abs(jnp.linspace(min_decay, max_decay, c, dtype=F32)).reshape(1, c)
    full = lambda shape: pl.BlockSpec(shape, lambda j, jc: (0,) * len(shape))
    return pl.pallas_call(
        functools.partial(_fft_filter_stage1_kernel, seq=l, segs=segs),
        grid=(FFT_B // segs, c // cb),
        in_specs=[
            pl.BlockSpec((segs, a, 2 * emb), lambda j, jc: (j, 0, 0)),
            full((2 * emb, 2 * fw)), full((1, 2 * fw)), full((2 * fw, 2 * fw)), full((1, 2 * fw)),
            full((2 * fw, 2 * fw)), full((1, 2 * fw)), full((3, 2 * fw)),
            pl.BlockSpec((HY_ORDER * 2, 2 * fw, cb), lambda j, jc: (0, 0, jc)),
            pl.BlockSpec((1, cb), lambda j, jc: (0, jc)),
            pl.BlockSpec((segs, rows, 2 * a), lambda j, jc: (j, 0, 0)),
        ],
        out_specs=pl.BlockSpec((HY_ORDER, segs, rows // 2, cb), lambda j, jc: (0, j, 0, jc)),
        out_shape=_sds((HY_ORDER, FFT_B, rows // 2, c), U32),
        scratch_shapes=[pltpu.VMEM((segs, a, 2 * fw), BF16)],
        compiler_params=_params("parallel", "arbitrary"),
        name="fft_filter_stage1",
    )(z2, w1d, both(b1), _block_diag2(w2).astype(BF16), both(b2), _block_diag2(w3).astype(BF16), both(b3),
      jnp.concatenate([freq, freq], axis=1), w4d, deltas, g1)


def _fft_mid_kernel(*refs, kq, nc):
    mf_ref, mi_ref = refs[0], refs[1]
    y_refs, k_refs = refs[2:2 + nc], refs[2 + nc:2 + 2 * nc]
    o_ref = refs[2 + 2 * nc]
    b = mf_ref.shape[0] // 2
    for q in range(kq):
        x = _dot(mf_ref[...], _stacked_bf16(_strided_rows(y_refs, q, b, kq)))
        k = _dot(mf_ref[...], _stacked_bf16(_strided_rows(k_refs, q, b, kq)))
        pr, pi = _cmul(x[:b], x[b:], k[:b], k[b:])
        o_ref[q] = _pack_stacked(_dot(mi_ref[...], jnp.concatenate([pr, pi], axis=0).astype(BF16)))


def _fft_mid(y3, yk4, order, m_fwd, m_inv):
    b, na, c = y3.shape
    kq = SUBLANES
    nc = c // LANES
    mat = pl.BlockSpec((2 * b, 2 * b), lambda i: (0, 0))
    y_specs = [pl.BlockSpec((b, kq, LANES), functools.partial(lambda i, q: (0, i, q), q=q)) for q in range(nc)]
    k_specs = [pl.BlockSpec((None, b, kq, LANES), functools.partial(lambda i, q: (order, 0, i, q), q=q))
               for q in range(nc)]
    return pl.pallas_call(
        functools.partial(_fft_mid_kernel, kq=kq, nc=nc),
        grid=(na // kq,),
        in_specs=[mat, mat] + y_specs + k_specs,
        out_specs=pl.BlockSpec((kq, b, c), lambda i: (i, 0, 0)),
        out_shape=_sds((na, b, c), U32),
        compiler_params=_params("parallel"),
        name="fft_mid",
    )(m_fwd, m_inv, *([y3] * nc), *([yk4] * nc))


def _fft_stage3_kernel(*refs, segs, nq, u_strided, head_dim):
    f_ref, v_refs = refs[0], refs[1:1 + nq]
    nu = nq if u_strided else 1
    u_refs, g_refs = refs[1 + nq:1 + nq + nu], refs[1 + nq + nu:1 + 2 * nq + nu]
    sk_ref, ng_ref, o_ref = refs[1 + 2 * nq + nu:]
    a = f_ref.shape[1]
    rows = f_ref.shape[2] // 2
    c = o_ref.shape[-1]
    for s in range(segs):
        y = _dot(f_ref[s], _stacked_bf16(_strided_rows(v_refs, s, rows, segs)))
        u = _strided_rows(u_refs, s, a, segs) if u_strided else u_refs[0][s]
        z = _strided_rows(g_refs, s, a, segs) * (y + u * sk_ref[...])
        if head_dim:
            heads = [z[:, h:h + head_dim] for h in range(0, c, head_dim)]
            heads = [zh * lax.rsqrt(jnp.mean(zh * zh, axis=-1, keepdims=True) + EPS) for zh in heads]
            z = jnp.concatenate(heads, axis=-1) * ng_ref[...]
        o_ref[s] = z


def _fft_stage3(v3, f3, u4, u_part, u_strided, g4, g_part, skip, norm_g, head_dim, cb=512):
    a = f3.shape[1]
    rows, b, c = v3.shape
    segs = SUBLANES
    cb = min(cb, c)
    nq = cb // LANES
    chunk = lambda shape, idx: [pl.BlockSpec(shape, functools.partial(idx, q=q)) for q in range(nq)]
    v_specs = chunk((rows, segs, LANES), lambda j, jc, q: (0, j, jc * nq + q))
    nat = lambda part: chunk((None, a, segs, LANES), lambda j, jc, q: (part, 0, j, jc * nq + q))
    u_specs = nat(u_part) if u_strided else [pl.BlockSpec((None, segs, a, cb), lambda j, jc: (u_part, j, 0, jc))]
    vec = pl.BlockSpec((1, cb), lambda j, jc: (0, jc))
    return pl.pallas_call(
        functools.partial(_fft_stage3_kernel, segs=segs, nq=nq, u_strided=u_strided, head_dim=head_dim),
        grid=(b // segs, c // cb),
        in_specs=[pl.BlockSpec((segs, a, 2 * rows), lambda j, jc: (j, 0, 0))] + v_specs + u_specs + nat(g_part)
        + [vec, vec],
        out_specs=pl.BlockSpec((segs, a, cb), lambda j, jc: (j, 0, jc)),
        out_shape=_sds((b, a, c), F32),
        compiler_params=_params("parallel", "parallel"),
        name="fft_stage3",
    )(f3, *([v3] * nq), *([u4] * (nq if u_strided else 1)), *([g4] * nq), skip.reshape(1, c), norm_g.reshape(1, c))


def _hyena_mixer(proj, conv_w, conv_b, filt, skip, norm_g, c):
    l = proj.shape[0]
    a = l // FFT_B
    hyc4 = _dwconv1d(proj, conv_w, conv_b, 3, c).reshape(3, a, FFT_B, c)
    m_fwd, m_inv = _stage2_matrices()
    g1, f3 = _stage1_matrices(l)
    yk = _fft_filter_stage1(l, c, *filt, g1)
    z4, part, strided = hyc4, 0, True
    for o in range(HY_ORDER):
        y3 = _fft_stage1(z4, part, g1, strided)
        v3 = _fft_mid(y3, yk, o, m_fwd, m_inv)
        last = o == HY_ORDER - 1
        z = _fft_stage3(v3, f3, z4, part, strided, hyc4, 1 + o, skip[o], norm_g, c // HY_HEADS if last else 0)
        z4, part, strided = z[None], 0, False
    return z


def _lower_bound(lg_ref, slot):
    lg = lg_ref[...]
    e = jnp.exp(lg - jnp.max(lg, axis=0, keepdims=True))
    sm = e / jnp.sum(e, axis=0, keepdims=True)
    return jnp.sum(sm[:slot + 1], axis=0, keepdims=True)


def _prefix_sum_rows(x):
    t = x.shape[0]
    row = lax.broadcasted_iota(jnp.int32, x.shape, 0)
    s = 1
    while s < t:
        x = x + jnp.where(row >= s, pltpu.roll(x, s, 0), 0.0)
        s *= 2
    return x


def _gla_kernel(*refs, slot, reverse, nchunk, chunk, hpb, with_out, combine):
    it = iter(refs)
    i_ref, f_ref, lg_ref, s0_ref = next(it), next(it), next(it), next(it)
    q_ref = next(it) if with_out else None
    ofw_ref, g_ref, ng_ref = (next(it), next(it), next(it)) if combine else (None, None, None)
    o_ref = next(it) if with_out else None
    sf_ref, st_ref = next(it), next(it)
    ke = HG_EXPAND

    @pl.when(pl.program_id(1) == 0)
    def _():
        st_ref[...] = s0_ref[...]

    lb = _lower_bound(lg_ref, slot)
    row = lax.broadcasted_iota(jnp.int32, (chunk, chunk), 0)
    col = lax.broadcasted_iota(jnp.int32, (chunk, chunk), 1)
    keep = (col >= row) if reverse else (col <= row)
    nt = (((1,), (1,)), ((), ()))
    tn = (((0,), (0,)), ((), ()))
    st = [st_ref[h] for h in range(hpb)]
    order = range(nchunk - 1, -1, -1) if reverse else range(nchunk)
    for ci in order:
        rows = slice(ci * chunk, (ci + 1) * chunk)
        f = lb + (1.0 - lb) * jax.nn.sigmoid(f_ref[rows, :])
        logf = jnp.log(f)
        key = 1.0 - f
        b = _prefix_sum_rows(logf)
        total = b[chunk - 1:chunk, :]
        if reverse:
            b = total - b + logf
        v = i_ref[rows, :].astype(BF16)
        k_state = (key * jnp.exp(total - b)).astype(BF16)
        decay = jnp.exp(total)
        if with_out:
            q = q_ref[rows, :]
            q_in = (q * jax.nn.sigmoid(q) * jnp.exp(b)).astype(BF16)
            k_intra = (key * jnp.exp(-b)).astype(BF16)
        outs = []
        for h in range(hpb):
            hl = slice(h * ke, (h + 1) * ke)
            if with_out:
                att = lax.dot_general(q_in[:, hl], k_intra[:, hl], nt, preferred_element_type=F32)
                att = jnp.where(keep, att, 0.0).astype(BF16)
                o = _dot(att, v[:, hl]) + lax.dot_general(q_in[:, hl], st[h].astype(BF16), nt,
                                                           preferred_element_type=F32)
                if combine:
                    o = o + ofw_ref[rows, hl]
                    o = o * lax.rsqrt(jnp.mean(o * o, axis=-1, keepdims=True) + EPS)
                outs.append(o)
            st[h] = st[h] * decay[:, hl] + lax.dot_general(v[:, hl], k_state[:, hl], tn, preferred_element_type=F32)
        if with_out:
            o = jnp.concatenate(outs, axis=-1)
            if combine:
                g = g_ref[rows, :]
                o = o * ng_ref[...] * (g * jax.nn.sigmoid(g))
            o_ref[rows, :] = o.astype(o_ref.dtype)
    for h in range(hpb):
        st_ref[h] = st[h]

    @pl.when(pl.program_id(1) == pl.num_programs(1) - 1)
    def _():
        for h in range(hpb):
            sf_ref[h] = st[h]


def _gla(src, cols, lb_logits, slot, s0, reverse, with_out, fwd_out=None, norm_g=None, out_dtype=F32, tb=512, hpb=8):
    l = src.shape[0]
    heads = s0.shape[0]
    ke = HG_EXPAND
    tb = min(tb, l)
    nblk = l // tb
    while heads % hpb or any(off % hpb for off in cols.values()):
        hpb //= 2
    wd = hpb * ke
    combine = fwd_out is not None
    blk = (lambda j: nblk - 1 - j) if reverse else (lambda j: j)
    col = lambda name: pl.BlockSpec((tb, wd), lambda h, j: (blk(j), cols[name] // hpb + h))
    head_rows = pl.BlockSpec((tb, wd), lambda h, j: (blk(j), h))
    state = pl.BlockSpec((hpb, ke, ke), lambda h, j: (h, 0, 0))
    in_specs = [col("i"), col("f"), pl.BlockSpec((lb_logits.shape[0], wd), lambda h, j: (0, h)), state]
    args = [src, src, lb_logits, s0]
    out_specs, out_shape = [], []
    if with_out:
        in_specs.append(col("q"))
        args.append(src)
        out_specs.append(head_rows)
        out_shape.append(_sds((l, heads * ke), out_dtype))
    if combine:
        in_specs += [head_rows, col("g"), pl.BlockSpec((1, wd), lambda h, j: (0, h))]
        args += [fwd_out, src, norm_g.reshape(1, heads * ke)]
    out_specs.append(state)
    out_shape.append(_sds((heads, ke, ke), F32))
    res = pl.pallas_call(
        functools.partial(_gla_kernel, slot=slot, reverse=reverse, nchunk=tb // HG_CHUNK, chunk=HG_CHUNK, hpb=hpb,
                          with_out=with_out, combine=combine),
        grid=(heads // hpb, nblk),
        in_specs=in_specs,
        out_specs=out_specs,
        out_shape=out_shape,
        scratch_shapes=[pltpu.VMEM((hpb, ke, ke), F32)],
        compiler_params=_params("parallel", "arbitrary"),
        name="hgrn_gla",
    )(*args)
    return res if with_out else (None, res[0])


def _hgrn_mixer(proj, q0, ctx_proj, lb_logits, slot, norm_g, heads):
    qb = q0 // HG_EXPAND
    zeros = jnp.zeros((heads, HG_EXPAND, HG_EXPAND), F32)
    _, s0_fw = _gla(ctx_proj, {"i": 0, "f": heads}, lb_logits, slot, zeros, False, False)
    _, s0_bw = _gla(ctx_proj, {"i": 0, "f": 2 * heads}, lb_logits, slot, zeros, True, False)
    cols = {"q": qb, "i": qb + heads, "f": qb + 2 * heads, "g": qb + 4 * heads}
    o_fw, _ = _gla(proj, cols, lb_logits, slot, s0_fw, False, True)
    cols["f"] = qb + 3 * heads
    y, _ = _gla(proj, cols, lb_logits, slot, s0_bw, True, True, fwd_out=o_fw, norm_g=norm_g, out_dtype=BF16)
    return y


def _outproj_kernel(*refs, na, nc):
    a_refs = refs[:nc]
    b_ref, w_ref, x_ref, g_ref, o_ref = refs[nc:]
    rows = a_refs[0].shape[0]
    ya =jnp.concatenate([_strided_rows(a_refs, s, rows, na) for s in range(na)], axis=0).astype(BF16)
    y = _dot(jnp.concatenate([ya, b_ref[...]], axis=1), w_ref[...])
    o_ref[...] = x_ref[...] + g_ref[...] * y


def _outproj(ya3, yb, w, x, gate, tn=1024):
    m, d = x.shape
    b, a, k1 = ya3.shape
    k2 = yb.shape[1]
    na = min(SUBLANES, a)
    tm = na * b
    tn = min(tn, d)
    nc = k1 // LANES
    a_specs = [pl.BlockSpec((b, na, LANES), functools.partial(lambda j, i, q: (0, i, q), q=q)) for q in range(nc)]
    return pl.pallas_call(
        functools.partial(_outproj_kernel, na=na, nc=nc),
        grid=(d // tn, m // tm),
        in_specs=a_specs + [
            pl.BlockSpec((tm, k2), lambda j, i: (i, 0)),
            pl.BlockSpec((k1 + k2, tn), lambda j, i: (0, j)),
            pl.BlockSpec((tm, tn), lambda j, i: (i, j)),
            pl.BlockSpec((1, tn), lambda j, i: (0, j)),
        ],
        out_specs=pl.BlockSpec((tm, tn), lambda j, i: (i, j)),
        out_shape=_sds((m, d), F32),
        compiler_params=_params("parallel", "parallel"),
        name="outproj_residual",
    )(*([ya3] * nc), yb, w, x, gate)


def _ffn_kernel(h_hbm, wa_ref, wu_ref, cw_ref, cb_ref, wd_ref, x_hbm, g_ref, fg_ref, o_ref,
                hext_ref, x_ref, act_ref, sem, *, gw, final):
    i, j = pl.program_id(0), pl.program_id(1)
    ni, nj = pl.num_programs(0), pl.num_programs(1) - 1
    tm = o_ref.shape[0]
    n = tm + 2 * gw
    row0 = i * tm

    def h_copy(src_row, dst_row, rows, k):
        return pltpu.make_async_copy(h_hbm.at[pl.ds(src_row, rows)], hext_ref.at[pl.ds(dst_row, rows)], sem.at[k])

    x_copy = pltpu.make_async_copy(x_hbm.at[pl.ds(row0, tm)], x_ref, sem.at[3])

    def h_block(blk, start):
        r0 = blk * tm
        act = (lambda c: c.start()) if start else (lambda c: c.wait())
        act(h_copy(r0, gw, tm, 0))

        @pl.when(blk > 0)
        def _():
            act(h_copy(r0 - gw, 0, gw, 1))

        @pl.when(blk < ni - 1)
        def _():
            act(h_copy(r0 + tm, gw + tm, gw, 2))

        if start:
            @pl.when(blk == 0)
            def _():
                hext_ref[0:gw, :] = jnp.zeros((gw, hext_ref.shape[1]), hext_ref.dtype)

            @pl.when(blk == ni - 1)
            def _():
                hext_ref[gw + tm:n, :] = jnp.zeros((gw, hext_ref.shape[1]), hext_ref.dtype)

    def up_part():
        a = _dot(hext_ref[...], wa_ref[...])
        u = _dot(hext_ref[gw:gw + tm, :], wu_ref[...])
        col = lax.broadcasted_iota(jnp.int32, a.shape, 0) % gw
        a_m1 = jnp.where(col == 0, 0.0, pltpu.roll(a, 1, 0))
        a_p1 = jnp.where(col == gw - 1, 0.0, pltpu.roll(a, n - 1, 0))
        conv = cb_ref[...]
        for di in range(3):
            rows = slice(di * gw, di * gw + tm)
            conv = conv + (a_m1[rows] * cw_ref[3 * di:3 * di + 1, :] + a[rows] * cw_ref[3 * di + 1:3 * di + 2, :]
                           + a_p1[rows] * cw_ref[3 * di + 2:3 * di + 3, :])
        gelu = 0.5 * conv * (1.0 + lax.erf(conv * (1.0 / math.sqrt(2.0))))
        act_ref[j % 2] = (gelu * u).astype(BF16)

    def down_part():
        o_ref[...] += _dot(act_ref[(j + 1) % 2], wd_ref[...])

    @pl.when(j == 0)
    def _():
        @pl.when(i == 0)
        def _():
            h_block(i, start=True)

        x_copy.start()
        o_ref[...] = jnp.zeros_like(o_ref)
        h_block(i, start=False)
        up_part()

    @pl.when(jnp.logical_and(j > 0, j < nj))
    def _():
        down_part()
        up_part()

    @pl.when(j == nj)
    def _():
        @pl.when(i < ni - 1)
        def _():
            h_block(i + 1, start=True)

        down_part()
        x_copy.wait()
        y = x_ref[...] + g_ref[...] * o_ref[...]
        if final:
            y = y * lax.rsqrt(jnp.mean(y * y, axis=-1, keepdims=True) + EPS) * fg_ref[...]
        o_ref[...] = y


def _conv_ffn(h, x, gate, w_up, w_down, layer, conv_w, conv_b, final_g, gw, tf=512):
    l, d = x.shape
    dff = w_down.shape[1]
    tm, tf = min(FFN_ROWS, l), min(tf, dff)
    nj = dff // tf
    final = final_g is not None
    fg = (final_g if final else jnp.ones((d,), F32)).reshape(1, d)
    hbm = pl.BlockSpec(memory_space=pl.ANY)
    up = lambda j: jnp.minimum(j, nj - 1)
    down = lambda j: jnp.maximum(j - 1, 0)
    return pl.pallas_call(
        functools.partial(_ffn_kernel, gw=gw, final=final),
        grid=(l // tm, nj + 1),
        in_specs=[
            hbm,
            pl.BlockSpec((None, d, tf), lambda i, j: (layer, 0, up(j))),
            pl.BlockSpec((None, d, tf), lambda i, j: (layer, 0, nj + up(j))),
            pl.BlockSpec((9, tf), lambda i, j: (0, up(j))),
            pl.BlockSpec((1, tf), lambda i, j: (0, up(j))),
            pl.BlockSpec((None, tf, d), lambda i, j: (layer, down(j), 0)),
            hbm,
            pl.BlockSpec((1, d), lambda i, j: (0, 0)),
            pl.BlockSpec((1, d), lambda i, j: (0, 0)),
        ],
        out_specs=pl.BlockSpec((tm, d), lambda i, j: (i, 0)),
        out_shape=_sds((l, d), F32),
        scratch_shapes=[pltpu.VMEM((tm + 2 * gw, d), BF16), pltpu.VMEM((tm, d), F32),
                        pltpu.VMEM((2, tm, tf), BF16), pltpu.SemaphoreType.DMA((4,))],
        compiler_params=_params("arbitrary", "arbitrary"),
        name="conv_ffn",
    )(h, w_up, w_up, conv_w.reshape(9, dff), conv_b.reshape(1, dff), w_down, x, gate, fg)


def _pool_kernel(xm_ref, xp_ref, xn_ref, ng_ref, sc_ref, sh_ref, w_ref, b_ref, ps_ref, g_ref, o_ref, *, seq,
                 windows):
    i = pl.program_id(0)
    tm = xm_ref.shape[0]
    pd = w_ref.shape[1]
    norm = lambda x: _rms_mod(x, ng_ref[...], sc_ref[...], sh_ref[...])
    xm = xm_ref[...]
    hm = norm(xm)
    hp = jnp.where(i > 0, norm(xp_ref[...]), 0.0)
    hn = jnp.where(i < pl.num_programs(0) - 1, norm(xn_ref[...]), 0.0)
    t = i * tm + lax.broadcasted_iota(jnp.int32, (tm, 1), 0)
    n = tm + 2 * POOL_HALO
    for gi, win in enumerate(windows):
        cols = slice(gi * pd, (gi + 1) * pd)
        e = jnp.concatenate([hp[:, cols], hm[:, cols], hn[:, cols]], axis=0)
        p = e + pltpu.roll(e, 1, 0)
        w = 2
        while w < win:
            p = pltpu.roll(p, w // 2, 0) + pltpu.roll(p, n - w // 2, 0)
            w *= 2
        count = (jnp.clip(t + win // 2, 0, seq) - jnp.clip(t - win // 2, 0, seq)).astype(F32)
        mean = p[POOL_HALO:POOL_HALO + tm] / count
        y = _dot((mean - hm[:, cols]).astype(BF16), w_ref[gi]) + b_ref[gi]
        o_ref[:, cols] = xm[:, cols] + g_ref[:, cols] * (y * ps_ref[:, cols])


def _pool_layer(x, norm_g, sc, sh, w, b, scale, gate, tm=512):
    l, d = x.shape
    ng, pd, _ = w.shape
    assert max(POOL_WINDOWS) // 2 <= POOL_HALO == SUBLANES
    tm = min(tm, l)
    r = tm // SUBLANES
    last = l // SUBLANES - 1
    row = pl.BlockSpec((1, d), lambda i: (0, 0))
    return pl.pallas_call(
        functools.partial(_pool_kernel, seq=l, windows=POOL_WINDOWS),
        grid=(l // tm,),
        in_specs=[
            pl.BlockSpec((tm, d), lambda i: (i, 0)),
            pl.BlockSpec((SUBLANES, d), lambda i: (jnp.maximum(i * r - 1, 0), 0)),
            pl.BlockSpec((SUBLANES, d), lambda i: (jnp.minimum((i + 1) * r, last), 0)),
            row, row, row,
            pl.BlockSpec((ng, pd, pd), lambda i: (0, 0, 0)),
            pl.BlockSpec((ng, 1, pd), lambda i: (0, 0, 0)),
            row, row,
        ],
        out_specs=pl.BlockSpec((tm, d), lambda i: (i, 0)),
        out_shape=_sds((l, d), F32),
        compiler_params=_params("parallel"),
        name="pool_mixer_residual",
    )(x, x, x, norm_g.reshape(1, d), sc, sh, w.astype(BF16), b.reshape(ng, 1, pd), scale.reshape(1, d), gate)


def kernel(x, c, ctx, c_ctx, norm_mix_g, norm_ffn_g, mod_w, mod_b, in_w, in_b, hy_conv_w, hy_conv_b, hy_w1, hy_b1, hy_w2, hy_b2, hy_w3, hy_b3, hy_freq, hy_w4, hy_skip, hy_norm_g, hg_lb_logits, hg_norm_g, out_w, pool_w, pool_b, pool_scale, ffn_up_w, ffn_conv_w, ffn_conv_b, ffn_down_w, final_norm_g):
    batch, _, d = x.shape
    depth = mod_w.shape[0]
    hy_d = hy_norm_g.shape[1]
    hg_d = hg_norm_g.shape[1]
    hg_q0 = 3 * hy_d
    hg_i0 = hg_q0 + hg_d
    hg_heads = hg_d // HG_EXPAND
    ffn_up_bf, ffn_down_bf = ffn_up_w.astype(BF16), ffn_down_w.astype(BF16)
    outs = []
    for bi in range(batch):
        xs = x[bi]
        cvecs = jnp.zeros((SUBLANES, d), F32).at[0].set(c[bi]).at[1].set(c_ctx)
        for l in range(depth):
            mod = _adaln(cvecs, mod_w, mod_b, l)
            sh1, sc1, g1, sh2, sc2, g2 = [mod[0:1, k * d:(k + 1) * d] for k in range(N_MOD)]
            if l % 2 == 0:
                e = l // 2
                in_w_bf = in_w[e].astype(BF16)
                proj = _norm_matmul(xs, norm_mix_g[l], sc1, sh1, in_w_bf, in_b[e])
                ctx_proj = _norm_matmul(ctx[bi], norm_mix_g[l], mod[1:2, d:2 * d], mod[1:2, 0:d],
                                        in_w_bf[:, hg_i0:hg_i0 + 3 * hg_d], in_b[e, hg_i0:hg_i0 + 3 * hg_d])
                filt = (hy_w1[e], hy_b1[e], hy_w2[e], hy_b2[e], hy_w3[e], hy_b3[e], hy_freq[e], hy_w4[e])
                y_hy = _hyena_mixer(proj, hy_conv_w[e], hy_conv_b[e], filt, hy_skip[e], hy_norm_g[e], hy_d)
                y_hg = _hgrn_mixer(proj, hg_q0, ctx_proj, hg_lb_logits, e, hg_norm_g[e], hg_heads)
                xs = _outproj(y_hy, y_hg, out_w[e].astype(BF16), xs, g1)
            else:
                od = l // 2
                xs = _pool_layer(xs, norm_mix_g[l], sc1, sh1, pool_w[od], pool_b[od], pool_scale[od], g1)
            h = _norm_mod(xs, norm_ffn_g[l], sc2, sh2)
            xs = _conv_ffn(h, xs, g2, ffn_up_bf, ffn_down_bf, l, ffn_conv_w[l], ffn_conv_b[l],
                           final_norm_g if l == depth - 1 else None, GRID_W)
        outs.append(xs[None])
    return outs[0] if batch == 1 else jnp.concatenate(outs, axis=0)
```

```python
import functools
import math

import numpy as np
import jax
import jax.numpy as jnp
from jax import lax
from jax.experimental import pallas as pl
from jax.experimental.pallas import tpu as pltpu

F32 = jnp.float32
BF16 = jnp.bfloat16
U32 = jnp.uint32
EPS = 1e-6

LANES = 128
SUBLANES = 8
VMEM_LIMIT_BYTES = 56 * 1024 * 1024

GRID_W = 64
N_MOD = 6
HY_HEADS = 8
HY_ORDER = 2
HY_EMB = 33
HY_DECAY_TARGET = 1e-2
HY_FAST_PCT = 0.3
HY_SLOW_PCT = 1.5
HG_EXPAND = 128
HG_CHUNK = 64
POOL_WINDOWS = (2, 4, 8, 16)
POOL_HALO = 8
FFT_B = 128
FFN_ROWS = 1024


def _params(*sem):
    return pltpu.CompilerParams(dimension_semantics=sem, vmem_limit_bytes=VMEM_LIMIT_BYTES)


def _sds(shape, dtype):
    return jax.ShapeDtypeStruct(shape, dtype)


def _dot(a, b):
    return jnp.dot(a, b, preferred_element_type=F32)


def _adaln_kernel(c_ref, w_ref, b_ref, o_ref):
    c = c_ref[...]
    s = (c * jax.nn.sigmoid(c)).astype(BF16)
    o_ref[...] = _dot(s, w_ref[...].astype(BF16)) + b_ref[...]


def _adaln(cvecs, mod_w, mod_b, layer, tn=1024):
    _, d, n = mod_w.shape
    tn = min(tn, n)
    return pl.pallas_call(
        _adaln_kernel,
        grid=(n // tn,),
        in_specs=[
            pl.BlockSpec((SUBLANES, d), lambda j: (0, 0)),
            pl.BlockSpec((None, d, tn), lambda j: (layer, 0, j)),
            pl.BlockSpec((None, 1, tn), lambda j: (layer, 0, j)),
        ],
        out_specs=pl.BlockSpec((SUBLANES, tn), lambda j: (0, j)),
        out_shape=_sds((SUBLANES, n), F32),
        compiler_params=_params("arbitrary"),
        name="adaln",
    )(cvecs, mod_w, mod_b.reshape(mod_b.shape[0], 1, n))


def _rms_mod(x, g, sc, sh):
    y = x * lax.rsqrt(jnp.mean(x * x, axis=-1, keepdims=True) + EPS) * g
    return y * (1.0 + sc) + sh


def _norm_mod_kernel(x_ref, g_ref, sc_ref, sh_ref, o_ref):
    o_ref[...] = _rms_mod(x_ref[...], g_ref[...], sc_ref[...], sh_ref[...]).astype(o_ref.dtype)


def _norm_mod(x, g, sc, sh, tm=1024):
    m, d = x.shape
    tm = min(tm, m)
    row = pl.BlockSpec((1, d), lambda i: (0, 0))
    return pl.pallas_call(
        _norm_mod_kernel,
        grid=(m // tm,),
        in_specs=[pl.BlockSpec((tm, d), lambda i: (i, 0)), row, row, row],
        out_specs=pl.BlockSpec((tm, d), lambda i: (i, 0)),
        out_shape=_sds((m, d), BF16),
        compiler_params=_params("parallel"),
        name="norm_mod",
    )(x, g.reshape(1, d), sc, sh)


def _norm_matmul_kernel(x_ref, g_ref, sc_ref, sh_ref, w_ref, b_ref, o_ref, h_ref, *, lookahead):
    i, j = pl.program_id(0), pl.program_id(1)
    last = pl.num_programs(1) - 1
    cur = i % 2

    def norm_into(slot):
        h_ref[slot] = _rms_mod(x_ref[...], g_ref[...], sc_ref[...], sh_ref[...]).astype(h_ref.dtype)

    def matmul():
        o_ref[...] = _dot(h_ref[cur], w_ref[...]) + b_ref[...]

    if not lookahead:
        @pl.when(j == 0)
        def _():
            norm_into(cur)

        matmul()
        return

    @pl.when(j < last)
    def _():
        @pl.when(jnp.logical_and(i == 0, j == 0))
        def _():
            norm_into(0)

        matmul()

    @pl.when(j == last)
    def _():
        norm_into(1 - cur)
        matmul()


def _norm_matmul(x, g, sc, sh, w, b, tm=1024, tn=1024):
    m, k = x.shape
    n = w.shape[1]
    tm, tn = min(tm, m), min(tn, n)
    ni, nj = m // tm, n // tn
    lookahead = nj > 1
    row = pl.BlockSpec((1, k), lambda i, j: (0, 0))
    x_block = ((lambda i, j: (jnp.minimum(jnp.where(j == nj - 1, i + 1, i), ni - 1), 0)) if lookahead
               else (lambda i, j: (i, 0)))
    return pl.pallas_call(
        functools.partial(_norm_matmul_kernel, lookahead=lookahead),
        grid=(ni, nj),
        in_specs=[
            pl.BlockSpec((tm, k), x_block), row, row, row,
            pl.BlockSpec((k, tn), lambda i, j: (0, j)),
            pl.BlockSpec((1, tn), lambda i, j: (0, j)),
        ],
        out_specs=pl.BlockSpec((tm, tn), lambda i, j: (i, j)),
        out_shape=_sds((m, n), F32),
        scratch_shapes=[pltpu.VMEM((2, tm, k), BF16)],
        compiler_params=_params("arbitrary", "arbitrary"),
        name="norm_matmul",
    )(x, g.reshape(1, k), sc, sh, w, b.reshape(1, n))


def _dwconv1d_kernel(xm_ref, xp_ref, xn_ref, w_ref, b_ref, o_ref):
    i = pl.program_id(0)
    x = xm_ref[...]
    tm = x.shape[0]
    prev_row = jnp.where(i > 0, xp_ref[SUBLANES - 1:SUBLANES, :], 0.0)
    next_row = jnp.where(i < pl.num_programs(0) - 1, xn_ref[0:1, :], 0.0)
    row = lax.broadcasted_iota(jnp.int32, x.shape, 0)
    x_m1 = jnp.where(row == 0, prev_row, pltpu.roll(x, 1, 0))
    x_p1 = jnp.where(row == tm - 1, next_row, pltpu.roll(x, tm - 1, 0))
    o_ref[...] = x_m1 * w_ref[0:1, :] + x * w_ref[1:2, :] + x_p1 * w_ref[2:3, :] + b_ref[...]


def _dwconv1d(proj, w, b, parts, c, tm=1024):
    l = proj.shape[0]
    tm = min(tm, l)
    r = tm // SUBLANES
    last = l // SUBLANES - 1
    return pl.pallas_call(
        _dwconv1d_kernel,
        grid=(l // tm, parts),
        in_specs=[
            pl.BlockSpec((tm, c), lambda i, j: (i, j)),
            pl.BlockSpec((SUBLANES, c), lambda i, j: (jnp.maximum(i * r - 1, 0), j)),
            pl.BlockSpec((SUBLANES, c), lambda i, j: (jnp.minimum((i + 1) * r, last), j)),
            pl.BlockSpec((3, c), lambda i, j: (0, j)),
            pl.BlockSpec((1, c), lambda i, j: (0, j)),
        ],
        out_specs=pl.BlockSpec((None, tm, c), lambda i, j: (j, i, 0)),
        out_shape=_sds((parts, l, c), F32),
        compiler_params=_params("parallel", "parallel"),
        name="hyena_dwconv1d",
    )(proj, proj, proj, w, b.reshape(1, parts * c))


def _hyena_pos_features(l):
    t = jnp.linspace(0.0, 1.0, l, dtype=F32)[:, None]
    bands = (HY_EMB - 1) // 2
    f = jnp.linspace(1e-4, bands - 1, bands, dtype=F32)[None, :]
    w = (2.0 * math.pi / l) * jnp.arange(l, dtype=F32)[:, None]
    return jnp.concatenate([t, jnp.cos(f * w), -jnp.sin(f * w)], axis=-1)


def _stage2_matrices():
    b = FFT_B
    angb = 2.0 * np.pi * np.outer(np.arange(b), np.arange(b)) / b
    cb, sb = np.cos(angb), np.sin(angb)
    m_fwd = np.block([[cb, sb], [-sb, cb]])
    m_inv = np.block([[cb, -sb], [sb, cb]])
    as_bf16 = lambda m: jnp.asarray(m, dtype=F32).astype(BF16)
    return as_bf16(m_fwd), as_bf16(m_inv)


def _stage1_matrices(l):
    b = FFT_B
    a = l // b
    na = 2 * a
    n = 2 * l
    idx = jnp.arange(na, dtype=jnp.int32)
    alpha = ((idx[:, None] * idx[None, :]) % na).astype(F32) * (2.0 * math.pi / na)
    beta = ((jnp.arange(b, dtype=jnp.int32)[:, None] * idx[None, :]) % n).astype(F32) * (2.0 * math.pi / n)
    ca, sa, cb, sb = jnp.cos(alpha), jnp.sin(alpha), jnp.cos(beta), jnp.sin(beta)
    cb1, sb1 = cb[:, :, None], sb[:, :, None]
    g1 = jnp.concatenate([ca[None] * cb1 - sa[None] * sb1, -(sa[None] * cb1 + ca[None] * sb1)], axis=1).astype(BF16)
    ca3, sa3 = ca[None, :a, :] * (1.0 / n), sa[None, :a, :] * (1.0 / n)
    cb3, sb3 = cb[:, None, :], sb[:, None, :]
    g3 = jnp.concatenate([ca3 * cb3 - sa3 * sb3, -(sa3 * cb3 + ca3 * sb3)], axis=2).astype(BF16)
    return g1, g3


def _cmul(ar, ai, br, bi):
    return ar * br - ai * bi, ar * bi + ai * br


def _pack_complex(re, im):
    r = lax.bitcast_convert_type(re.astype(BF16).astype(F32), U32)
    i = lax.bitcast_convert_type(im.astype(BF16).astype(F32), U32)
    return r | (i >> 16)


def _unpack_complex(w):
    re = lax.bitcast_convert_type(w & jnp.uint32(0xFFFF0000), F32)
    im = lax.bitcast_convert_type(w << 16, F32)
    return re, im


def _stacked_bf16(w):
    re, im = _unpack_complex(w)
    return jnp.concatenate([re, im], axis=0).astype(BF16)


def _strided_rows(chunk_refs, s, rows, stride):
    flat = [r.reshape(rows * stride, LANES) for r in chunk_refs]
    return jnp.concatenate([r[pl.ds(s, rows, stride=stride), :] for r in flat], axis=-1)


def _pack_stacked(y):
    half = y.shape[0] // 2
    return _pack_complex(y[:half], y[half:])


def _fft_stage1_kernel(*refs, segs, nq, strided):
    f_ref, x_refs, o_ref = refs[0], refs[1:1 + nq], refs[1 + nq]
    ka = f_ref.shape[2]
    for s in range(segs):
        x = _strided_rows(x_refs, s, ka, segs) if strided else x_refs[0][s]
        o_ref[s] = _pack_stacked(_dot(f_ref[s], x.astype(BF16)))


def _fft_stage1(x4, part, g1, strided, cb=512):
    if strided:
        _, ka, b, c = x4.shape
    else:
        _, b, ka, c = x4.shape
    rows = g1.shape[1]
    segs = SUBLANES
    cb = min(cb, c)
    if strided:
        nq = cb // LANES
        x_specs = [pl.BlockSpec((None, ka, segs, LANES), functools.partial(lambda j, jc, q: (part, 0, j, jc * nq + q), q=q))
                   for q in range(nq)]
    else:
        nq = 1
        x_specs = [pl.BlockSpec((None, segs, ka, cb), lambda j, jc: (part, j, 0, jc))]
    return pl.pallas_call(
        functools.partial(_fft_stage1_kernel, segs=segs, nq=nq, strided=strided),
        grid=(b // segs, c // cb),
        in_specs=[pl.BlockSpec((segs, rows, ka), lambda j, jc: (j, 0, 0))] + x_specs,
        out_specs=pl.BlockSpec((segs, rows // 2, cb), lambda j, jc: (j, 0, jc)),
        out_shape=_sds((b, rows // 2, c), U32),
        compiler_params=_params("parallel", "parallel"),
        name="fft_stage1",
    )(g1, *([x4] * nq))


def _fft_filter_stage1_kernel(z_ref, w1_ref, b1_ref, w2_ref, b2_ref, w3_ref, b3_ref, fr_ref, w4_ref, dl_ref, f_ref,
                              o_ref, h_ref, *, seq, segs):
    a_n = z_ref.shape[1]
    inv = 1.0 / (seq - 1)

    @pl.when(pl.program_id(1) == 0)
    def _():
        for s in range(segs):
            h = jnp.sin(fr_ref[0:1, :] * (_dot(z_ref[s], w1_ref[...]) + b1_ref[...]))
            h = jnp.sin(fr_ref[1:2, :] * (_dot(h.astype(BF16), w2_ref[...]) + b2_ref[...]))
            h = jnp.sin(fr_ref[2:3, :] * (_dot(h.astype(BF16), w3_ref[...]) + b3_ref[...]))
            h_ref[s] = h.astype(BF16)

    for s in range(segs):
        b = pl.program_id(0) * segs + s
        n = (lax.broadcasted_iota(jnp.int32, (a_n, 1), 0) * FFT_B + b).astype(F32)
        e_fwd = jnp.exp(-(n * inv) * dl_ref[...])
        e_rev = jnp.where(n == 0.0, 0.0, jnp.exp(-((seq - n) * inv) * dl_ref[...]))
        h = h_ref[s]
        for o in range(HY_ORDER):
            k = jnp.concatenate([_dot(h, w4_ref[2 * o]) * e_fwd, _dot(h, w4_ref[2 * o + 1]) * e_rev], axis=0)
            o_ref[o, s] = _pack_stacked(_dot(f_ref[s], k.astype(BF16)))


def _block_diag2(w):
    z = jnp.zeros_like(w)
    return jnp.concatenate([jnp.concatenate([w, z], axis=1), jnp.concatenate([z, w], axis=1)], axis=0)


def _fft_filter_stage1(l, c, w1, b1, w2, b2, w3, b3, freq, w4, g1, cb=256):
    fw = w1.shape[1]
    emb = 64
    a = l // FFT_B
    rows = g1.shape[1]
    segs = SUBLANES
    cb = min(cb, c)
    z = _hyena_pos_features(l)
    z_rev = jnp.concatenate([z[:1], z[:0:-1]], axis=0)
    pad = lambda t: jnp.pad(t, ((0, 0), (0, emb - HY_EMB)))
    z2 = jnp.concatenate([pad(z), pad(z_rev)], axis=1).astype(BF16)
    z2 = z2.reshape(a, FFT_B, 2 * emb).transpose(1, 0, 2)
    both = lambda v: jnp.concatenate([v, v]).reshape(1, 2 * fw)
    w1d = _block_diag2(jnp.pad(w1, ((0, emb - HY_EMB), (0, 0)))).astype(BF16)
    w4r = w4.reshape(fw, HY_ORDER * 2, c).transpose(1, 0, 2)
    zeros = jnp.zeros_like(w4r)
    fwd_dir = (jnp.arange(HY_ORDER * 2) % 2 == 0)[:, None, None]
    w4d = jnp.concatenate([jnp.where(fwd_dir, w4r, zeros), jnp.where(fwd_dir, zeros, w4r)], axis=1).astype(BF16)
    max_decay = math.log(HY_DECAY_TARGET) / HY_FAST_PCT
    min_decay = math.log(HY_DECAY_TARGET) / HY_SLOW_PCT
    deltas = jnp.abs(jnp.linspace(min_decay, max_decay, c, dtype=F32)).reshape(1, c)
    full = lambda shape: pl.BlockSpec(shape, lambda j, jc: (0,) * len(shape))
    return pl.pallas_call(
        functools.partial(_fft_filter_stage1_kernel, seq=l, segs=segs),
        grid=(FFT_B // segs, c // cb),
        in_specs=[
            pl.BlockSpec((segs, a, 2 * emb), lambda j, jc: (j, 0, 0)),
            full((2 * emb, 2 * fw)), full((1, 2 * fw)), full((2 * fw, 2 * fw)), full((1, 2 * fw)),
            full((2 * fw, 2 * fw)), full((1, 2 * fw)), full((3, 2 * fw)),
            pl.BlockSpec((HY_ORDER * 2, 2 * fw, cb), lambda j, jc: (0, 0, jc)),
            pl.BlockSpec((1, cb), lambda j, jc: (0, jc)),
            pl.BlockSpec((segs, rows, 2 * a), lambda j, jc: (j, 0, 0)),
        ],
        out_specs=pl.BlockSpec((HY_ORDER, segs, rows // 2, cb), lambda j, jc: (0, j, 0, jc)),
        out_shape=_sds((HY_ORDER, FFT_B, rows // 2, c), U32),
        scratch_shapes=[pltpu.VMEM((segs, a, 2 * fw), BF16)],
        compiler_params=_params("parallel", "arbitrary"),
        name="fft_filter_stage1",
    )(z2, w1d, both(b1), _block_diag2(w2).astype(BF16), both(b2), _block_diag2(w3).astype(BF16), both(b3),
      jnp.concatenate([freq, freq], axis=1), w4d, deltas, g1)


def _fft_mid_kernel(*refs, kq, nc):
    mf_ref, mi_ref = refs[0], refs[1]
    y_refs, k_refs = refs[2:2 + nc], refs[2 + nc:2 + 2 * nc]
    o_ref = refs[2 + 2 * nc]
    b = mf_ref.shape[0] // 2
    for q in range(kq):
        x = _dot(mf_ref[...], _stacked_bf16(_strided_rows(y_refs, q, b, kq)))
        k = _dot(mf_ref[...], _stacked_bf16(_strided_rows(k_refs, q, b, kq)))
        pr, pi = _cmul(x[:b], x[b:], k[:b], k[b:])
        o_ref[q] = _pack_stacked(_dot(mi_ref[...], jnp.concatenate([pr, pi], axis=0).astype(BF16)))


def _fft_mid(y3, yk4, order, m_fwd, m_inv):
    b, na, c = y3.shape
    kq = SUBLANES
    nc = c // LANES
    mat = pl.BlockSpec((2 * b, 2 * b), lambda i: (0, 0))
    y_specs = [pl.BlockSpec((b, kq, LANES), functools.partial(lambda i, q: (0, i, q), q=q)) for q in range(nc)]
    k_specs = [pl.BlockSpec((None, b, kq, LANES), functools.partial(lambda i, q: (order, 0, i, q), q=q))
               for q in range(nc)]
    return pl.pallas_call(
        functools.partial(_fft_mid_kernel, kq=kq, nc=nc),
        grid=(na // kq,),
        in_specs=[mat, mat] + y_specs + k_specs,
        out_specs=pl.BlockSpec((kq, b, c), lambda i: (i, 0, 0)),
        out_shape=_sds((na, b, c), U32),
        compiler_params=_params("parallel"),
        name="fft_mid",
    )(m_fwd, m_inv, *([y3] * nc), *([yk4] * nc))


def _fft_stage3_kernel(*refs, segs, nq, u_strided, head_dim):
    f_ref, v_refs = refs[0], refs[1:1 + nq]
    nu = nq if u_strided else 1
    u_refs, g_refs = refs[1 + nq:1 + nq + nu], refs[1 + nq + nu:1 + 2 * nq + nu]
    sk_ref, ng_ref, o_ref = refs[1 + 2 * nq + nu:]
    a = f_ref.shape[1]
    rows = f_ref.shape[2] // 2
    c = o_ref.shape[-1]
    for s in range(segs):
        y = _dot(f_ref[s], _stacked_bf16(_strided_rows(v_refs, s, rows, segs)))
        u = _strided_rows(u_refs, s, a, segs) if u_strided else u_refs[0][s]
        z = _strided_rows(g_refs, s, a, segs) * (y + u * sk_ref[...])
        if head_dim:
            heads = [z[:, h:h + head_dim] for h in range(0, c, head_dim)]
            heads = [zh * lax.rsqrt(jnp.mean(zh * zh, axis=-1, keepdims=True) + EPS) for zh in heads]
            z = jnp.concatenate(heads, axis=-1) * ng_ref[...]
        o_ref[s] = z


def _fft_stage3(v3, f3, u4, u_part, u_strided, g4, g_part, skip, norm_g, head_dim, cb=512):
    a = f3.shape[1]
    rows, b, c = v3.shape
    segs = SUBLANES
    cb = min(cb, c)
    nq = cb // LANES
    chunk = lambda shape, idx: [pl.BlockSpec(shape, functools.partial(idx, q=q)) for q in range(nq)]
    v_specs = chunk((rows, segs, LANES), lambda j, jc, q: (0, j, jc * nq + q))
    nat = lambda part: chunk((None, a, segs, LANES), lambda j, jc, q: (part, 0, j, jc * nq + q))
    u_specs = nat(u_part) if u_strided else [pl.BlockSpec((None, segs, a, cb), lambda j, jc: (u_part, j, 0, jc))]
    vec = pl.BlockSpec((1, cb), lambda j, jc: (0, jc))
    return pl.pallas_call(
        functools.partial(_fft_stage3_kernel, segs=segs, nq=nq, u_strided=u_strided, head_dim=head_dim),
        grid=(b // segs, c // cb),
        in_specs=[pl.BlockSpec((segs, a, 2 * rows), lambda j, jc: (j, 0, 0))] + v_specs + u_specs + nat(g_part)
        + [vec, vec],
        out_specs=pl.BlockSpec((segs, a, cb), lambda j, jc: (j, 0, jc)),
        out_shape=_sds((b, a, c), F32),
        compiler_params=_params("parallel", "parallel"),
        name="fft_stage3",
    )(f3, *([v3] * nq), *([u4] * (nq if u_strided else 1)), *([g4] * nq), skip.reshape(1, c), norm_g.reshape(1, c))


def _hyena_mixer(proj, conv_w, conv_b, filt, skip, norm_g, c):
    l = proj.shape[0]
    a = l // FFT_B
    hyc4 = _dwconv1d(proj, conv_w, conv_b, 3, c).reshape(3, a, FFT_B, c)
    m_fwd, m_inv = _stage2_matrices()
    g1, f3 = _stage1_matrices(l)
    yk = _fft_filter_stage1(l, c, *filt, g1)
    z4, part, strided = hyc4, 0, True
    for o in range(HY_ORDER):
        y3 = _fft_stage1(z4, part, g1, strided)
        v3 = _fft_mid(y3, yk, o, m_fwd, m_inv)
        last = o == HY_ORDER - 1
        z = _fft_stage3(v3, f3, z4, part, strided, hyc4, 1 + o, skip[o], norm_g, c // HY_HEADS if last else 0)
        z4, part, strided = z[None], 0, False
    return z


def _lower_bound(lg_ref, slot):
    lg = lg_ref[...]
    e = jnp.exp(lg - jnp.max(lg, axis=0, keepdims=True))
    sm = e / jnp.sum(e, axis=0, keepdims=True)
    return jnp.sum(sm[:slot + 1], axis=0, keepdims=True)


def _prefix_sum_rows(x):
    t = x.shape[0]
    row = lax.broadcasted_iota(jnp.int32, x.shape, 0)
    s = 1
    while s < t:
        x = x + jnp.where(row >= s, pltpu.roll(x, s, 0), 0.0)
        s *= 2
    return x


def _gla_kernel(*refs, slot, reverse, nchunk, chunk, hpb, with_out, combine):
    it = iter(refs)
    i_ref, f_ref, lg_ref, s0_ref = next(it), next(it), next(it), next(it)
    q_ref = next(it) if with_out else None
    ofw_ref, g_ref, ng_ref = (next(it), next(it), next(it)) if combine else (None, None, None)
    o_ref = next(it) if with_out else None
    sf_ref, st_ref = next(it), next(it)
    ke = HG_EXPAND

    @pl.when(pl.program_id(1) == 0)
    def _():
        st_ref[...] = s0_ref[...]

    lb = _lower_bound(lg_ref, slot)
    row = lax.broadcasted_iota(jnp.int32, (chunk, chunk), 0)
    col = lax.broadcasted_iota(jnp.int32, (chunk, chunk), 1)
    keep = (col >= row) if reverse else (col <= row)
    nt = (((1,), (1,)), ((), ()))
    tn = (((0,), (0,)), ((), ()))
    st = [st_ref[h] for h in range(hpb)]
    order = range(nchunk - 1, -1, -1) if reverse else range(nchunk)
    for ci in order:
        rows = slice(ci * chunk, (ci + 1) * chunk)
        f = lb + (1.0 - lb) * jax.nn.sigmoid(f_ref[rows, :])
        logf = jnp.log(f)
        key = 1.0 - f
        b = _prefix_sum_rows(logf)
        total = b[chunk - 1:chunk, :]
        if reverse:
            b = total - b + logf
        v = i_ref[rows, :].astype(BF16)
        k_state = (key * jnp.exp(total - b)).astype(BF16)
        decay = jnp.exp(total)
        if with_out:
            q = q_ref[rows, :]
            q_in = (q * jax.nn.sigmoid(q) * jnp.exp(b)).astype(BF16)
            k_intra = (key * jnp.exp(-b)).astype(BF16)
        outs = []
        for h in range(hpb):
            hl = slice(h * ke, (h + 1) * ke)
            if with_out:
                att = lax.dot_general(q_in[:, hl], k_intra[:, hl], nt, preferred_element_type=F32)
                att = jnp.where(keep, att, 0.0).astype(BF16)
                o = _dot(att, v[:, hl]) + lax.dot_general(q_in[:, hl], st[h].astype(BF16), nt,
                                                           preferred_element_type=F32)
                if combine:
                    o = o + ofw_ref[rows, hl]
                    o = o * lax.rsqrt(jnp.mean(o * o, axis=-1, keepdims=True) + EPS)
                outs.append(o)
            st[h] = st[h] * decay[:, hl] + lax.dot_general(v[:, hl], k_state[:, hl], tn, preferred_element_type=F32)
        if with_out:
            o = jnp.concatenate(outs, axis=-1)
            if combine:
                g = g_ref[rows, :]
                o = o * ng_ref[...] * (g * jax.nn.sigmoid(g))
            o_ref[rows, :] = o.astype(o_ref.dtype)
    for h in range(hpb):
        st_ref[h] = st[h]

    @pl.when(pl.program_id(1) == pl.num_programs(1) - 1)
    def _():
        for h in range(hpb):
            sf_ref[h] = st[h]


def _gla(src, cols, lb_logits, slot, s0, reverse, with_out, fwd_out=None, norm_g=None, out_dtype=F32, tb=512, hpb=8):
    l = src.shape[0]
    heads = s0.shape[0]
    ke = HG_EXPAND
    tb = min(tb, l)
    nblk = l // tb
    while heads % hpb or any(off % hpb for off in cols.values()):
        hpb //= 2
    wd = hpb * ke
    combine = fwd_out is not None
    blk = (lambda j: nblk - 1 - j) if reverse else (lambda j: j)
    col = lambda name: pl.BlockSpec((tb, wd), lambda h, j: (blk(j), cols[name] // hpb + h))
    head_rows = pl.BlockSpec((tb, wd), lambda h, j: (blk(j), h))
    state = pl.BlockSpec((hpb, ke, ke), lambda h, j: (h, 0, 0))
    in_specs = [col("i"), col("f"), pl.BlockSpec((lb_logits.shape[0], wd), lambda h, j: (0, h)), state]
    args = [src, src, lb_logits, s0]
    out_specs, out_shape = [], []
    if with_out:
        in_specs.append(col("q"))
        args.append(src)
        out_specs.append(head_rows)
        out_shape.append(_sds((l, heads * ke), out_dtype))
    if combine:
        in_specs += [head_rows, col("g"), pl.BlockSpec((1, wd), lambda h, j: (0, h))]
        args += [fwd_out, src, norm_g.reshape(1, heads * ke)]
    out_specs.append(state)
    out_shape.append(_sds((heads, ke, ke), F32))
    res = pl.pallas_call(
        functools.partial(_gla_kernel, slot=slot, reverse=reverse, nchunk=tb // HG_CHUNK, chunk=HG_CHUNK, hpb=hpb,
                          with_out=with_out, combine=combine),
        grid=(heads // hpb, nblk),
        in_specs=in_specs,
        out_specs=out_specs,
        out_shape=out_shape,
        scratch_shapes=[pltpu.VMEM((hpb, ke, ke), F32)],
        compiler_params=_params("parallel", "arbitrary"),
        name="hgrn_gla",
    )(*args)
    return res if with_out else (None, res[0])


def _hgrn_mixer(proj, q0, ctx_proj, lb_logits, slot, norm_g, heads):
    qb = q0 // HG_EXPAND
    zeros = jnp.zeros((heads, HG_EXPAND, HG_EXPAND), F32)
    _, s0_fw = _gla(ctx_proj, {"i": 0, "f": heads}, lb_logits, slot, zeros, False, False)
    _, s0_bw = _gla(ctx_proj, {"i": 0, "f": 2 * heads}, lb_logits, slot, zeros, True, False)
    cols = {"q": qb, "i": qb + heads, "f": qb + 2 * heads, "g": qb + 4 * heads}
    o_fw, _ = _gla(proj, cols, lb_logits, slot, s0_fw, False, True)
    cols["f"] = qb + 3 * heads
    y, _ = _gla(proj, cols, lb_logits, slot, s0_bw, True, True, fwd_out=o_fw, norm_g=norm_g, out_dtype=BF16)
    return y


def _outproj_kernel(*refs, na, nc):
    a_refs = refs[:nc]
    b_ref, w_ref, x_ref, g_ref, o_ref = refs[nc:]
    rows = a_refs[0].shape[0]
    ya =jnp.concatenate([_strided_rows(a_refs, s, rows, na) for s in range(na)], axis=0).astype(BF16)
    y = _dot(jnp.concatenate([ya, b_ref[...]], axis=1), w_ref[...])
    o_ref[...] = x_ref[...] + g_ref[...] * y


def _outproj(ya3, yb, w, x, gate, tn=1024):
    m, d = x.shape
    b, a, k1 = ya3.shape
    k2 = yb.shape[1]
    na = min(SUBLANES, a)
    tm = na * b
    tn = min(tn, d)
    nc = k1 // LANES
    a_specs = [pl.BlockSpec((b, na, LANES), functools.partial(lambda j, i, q: (0, i, q), q=q)) for q in range(nc)]
    return pl.pallas_call(
        functools.partial(_outproj_kernel, na=na, nc=nc),
        grid=(d // tn, m // tm),
        in_specs=a_specs + [
            pl.BlockSpec((tm, k2), lambda j, i: (i, 0)),
            pl.BlockSpec((k1 + k2, tn), lambda j, i: (0, j)),
            pl.BlockSpec((tm, tn), lambda j, i: (i, j)),
            pl.BlockSpec((1, tn), lambda j, i: (0, j)),
        ],
        out_specs=pl.BlockSpec((tm, tn), lambda j, i: (i, j)),
        out_shape=_sds((m, d), F32),
        compiler_params=_params("parallel", "parallel"),
        name="outproj_residual",
    )(*([ya3] * nc), yb, w, x, gate)


def _ffn_kernel(h_hbm, wa_ref, wu_ref, cw_ref, cb_ref, wd_ref, x_hbm, g_ref, fg_ref, o_ref,
                hext_ref, x_ref, act_ref, sem, *, gw, final):
    i, j = pl.program_id(0), pl.program_id(1)
    ni, nj = pl.num_programs(0), pl.num_programs(1) - 1
    tm = o_ref.shape[0]
    n = tm + 2 * gw
    row0 = i * tm

    def h_copy(src_row, dst_row, rows, k):
        return pltpu.make_async_copy(h_hbm.at[pl.ds(src_row, rows)], hext_ref.at[pl.ds(dst_row, rows)], sem.at[k])

    x_copy = pltpu.make_async_copy(x_hbm.at[pl.ds(row0, tm)], x_ref, sem.at[3])

    def h_block(blk, start):
        r0 = blk * tm
        act = (lambda c: c.start()) if start else (lambda c: c.wait())
        act(h_copy(r0, gw, tm, 0))

        @pl.when(blk > 0)
        def _():
            act(h_copy(r0 - gw, 0, gw, 1))

        @pl.when(blk < ni - 1)
        def _():
            act(h_copy(r0 + tm, gw + tm, gw, 2))

        if start:
            @pl.when(blk == 0)
            def _():
                hext_ref[0:gw, :] = jnp.zeros((gw, hext_ref.shape[1]), hext_ref.dtype)

            @pl.when(blk == ni - 1)
            def _():
                hext_ref[gw + tm:n, :] = jnp.zeros((gw, hext_ref.shape[1]), hext_ref.dtype)

    def up_part():
        a = _dot(hext_ref[...], wa_ref[...])
        u = _dot(hext_ref[gw:gw + tm, :], wu_ref[...])
        col = lax.broadcasted_iota(jnp.int32, a.shape, 0) % gw
        a_m1 = jnp.where(col == 0, 0.0, pltpu.roll(a, 1, 0))
        a_p1 = jnp.where(col == gw - 1, 0.0, pltpu.roll(a, n - 1, 0))
        conv = cb_ref[...]
        for di in range(3):
            rows = slice(di * gw, di * gw + tm)
            conv = conv + (a_m1[rows] * cw_ref[3 * di:3 * di + 1, :] + a[rows] * cw_ref[3 * di + 1:3 * di + 2, :]
                           + a_p1[rows] * cw_ref[3 * di + 2:3 * di + 3, :])
        gelu = 0.5 * conv * (1.0 + lax.erf(conv * (1.0 / math.sqrt(2.0))))
        act_ref[j % 2] = (gelu * u).astype(BF16)

    def down_part():
        o_ref[...] += _dot(act_ref[(j + 1) % 2], wd_ref[...])

    @pl.when(j == 0)
    def _():
        @pl.when(i == 0)
        def _():
            h_block(i, start=True)

        x_copy.start()
        o_ref[...] = jnp.zeros_like(o_ref)
        h_block(i, start=False)
        up_part()

    @pl.when(jnp.logical_and(j > 0, j < nj))
    def _():
        down_part()
        up_part()

    @pl.when(j == nj)
    def _():
        @pl.when(i < ni - 1)
        def _():
            h_block(i + 1, start=True)

        down_part()
        x_copy.wait()
        y = x_ref[...] + g_ref[...] * o_ref[...]
        if final:
            y = y * lax.rsqrt(jnp.mean(y * y, axis=-1, keepdims=True) + EPS) * fg_ref[...]
        o_ref[...] = y


def _conv_ffn(h, x, gate, w_up, w_down, layer, conv_w, conv_b, final_g, gw, tf=512):
    l, d = x.shape
    dff = w_down.shape[1]
    tm, tf = min(FFN_ROWS, l), min(tf, dff)
    nj = dff // tf
    final = final_g is not None
    fg = (final_g if final else jnp.ones((d,), F32)).reshape(1, d)
    hbm = pl.BlockSpec(memory_space=pl.ANY)
    up = lambda j: jnp.minimum(j, nj - 1)
    down = lambda j: jnp.maximum(j - 1, 0)
    return pl.pallas_call(
        functools.partial(_ffn_kernel, gw=gw, final=final),
        grid=(l // tm, nj + 1),
        in_specs=[
            hbm,
            pl.BlockSpec((None, d, tf), lambda i, j: (layer, 0, up(j))),
            pl.BlockSpec((None, d, tf), lambda i, j: (layer, 0, nj + up(j))),
            pl.BlockSpec((9, tf), lambda i, j: (0, up(j))),
            pl.BlockSpec((1, tf), lambda i, j: (0, up(j))),
            pl.BlockSpec((None, tf, d), lambda i, j: (layer, down(j), 0)),
            hbm,
            pl.BlockSpec((1, d), lambda i, j: (0, 0)),
            pl.BlockSpec((1, d), lambda i, j: (0, 0)),
        ],
        out_specs=pl.BlockSpec((tm, d), lambda i, j: (i, 0)),
        out_shape=_sds((l, d), F32),
        scratch_shapes=[pltpu.VMEM((tm + 2 * gw, d), BF16), pltpu.VMEM((tm, d), F32),
                        pltpu.VMEM((2, tm, tf), BF16), pltpu.SemaphoreType.DMA((4,))],
        compiler_params=_params("arbitrary", "arbitrary"),
        name="conv_ffn",
    )(h, w_up, w_up, conv_w.reshape(9, dff), conv_b.reshape(1, dff), w_down, x, gate, fg)


def _pool_kernel(xm_ref, xp_ref, xn_ref, ng_ref, sc_ref, sh_ref, w_ref, b_ref, ps_ref, g_ref, ng2_ref, sc2_ref,
                 sh2_ref, o_ref, h_ref, *, seq, windows):
    i = pl.program_id(0)
    tm = xm_ref.shape[0]
    pd = w_ref.shape[1]
    norm = lambda x: _rms_mod(x, ng_ref[...], sc_ref[...], sh_ref[...])
    xm = xm_ref[...]
    hm = norm(xm)
    hp = jnp.where(i > 0, norm(xp_ref[...]), 0.0)
    hn = jnp.where(i < pl.num_programs(0) - 1, norm(xn_ref[...]), 0.0)
    t = i * tm + lax.broadcasted_iota(jnp.int32, (tm, 1), 0)
    n = tm + 2 * POOL_HALO
    for gi, win in enumerate(windows):
        cols = slice(gi * pd, (gi + 1) * pd)
        e = jnp.concatenate([hp[:, cols], hm[:, cols], hn[:, cols]], axis=0)
        p = e + pltpu.roll(e, 1, 0)
        w = 2
        while w < win:
            p = pltpu.roll(p, w // 2, 0) + pltpu.roll(p, n - w // 2, 0)
            w *= 2
        count = (jnp.clip(t + win // 2, 0, seq) - jnp.clip(t - win // 2, 0, seq)).astype(F32)
        mean = p[POOL_HALO:POOL_HALO + tm] / count
        y = _dot((mean - hm[:, cols]).astype(BF16), w_ref[gi]) + b_ref[gi]
        o_ref[:, cols] = xm[:, cols] + g_ref[:, cols] * (y * ps_ref[:, cols])
    h_ref[...] = _rms_mod(o_ref[...], ng2_ref[...], sc2_ref[...], sh2_ref[...]).astype(h_ref.dtype)


def _pool_layer(x, norm_g, sc, sh, w, b, scale, gate, norm2_g, sc2, sh2, tm=512):
    l, d = x.shape
    ng, pd, _ = w.shape
    assert max(POOL_WINDOWS) // 2 <= POOL_HALO == SUBLANES
    tm = min(tm, l)
    r = tm // SUBLANES
    last = l // SUBLANES - 1
    row = pl.BlockSpec((1, d), lambda i: (0, 0))
    return pl.pallas_call(
        functools.partial(_pool_kernel, seq=l, windows=POOL_WINDOWS),
        grid=(l // tm,),
        in_specs=[
            pl.BlockSpec((tm, d), lambda i: (i, 0)),
            pl.BlockSpec((SUBLANES, d), lambda i: (jnp.maximum(i * r - 1, 0), 0)),
            pl.BlockSpec((SUBLANES, d), lambda i: (jnp.minimum((i + 1) * r, last), 0)),
            row, row, row,
            pl.BlockSpec((ng, pd, pd), lambda i: (0, 0, 0)),
            pl.BlockSpec((ng, 1, pd), lambda i: (0, 0, 0)),
            row, row, row, row, row,
        ],
        out_specs=[pl.BlockSpec((tm, d), lambda i: (i, 0)), pl.BlockSpec((tm, d), lambda i: (i, 0))],
        out_shape=[_sds((l, d), F32), _sds((l, d), BF16)],
        compiler_params=_params("parallel"),
        name="pool_mixer_residual",
    )(x, x, x, norm_g.reshape(1, d), sc, sh, w.astype(BF16), b.reshape(ng, 1, pd), scale.reshape(1, d), gate,
      norm2_g.reshape(1, d), sc2, sh2)


def kernel(x, c, ctx, c_ctx, norm_mix_g, norm_ffn_g, mod_w, mod_b, in_w, in_b, hy_conv_w, hy_conv_b, hy_w1, hy_b1, hy_w2, hy_b2, hy_w3, hy_b3, hy_freq, hy_w4, hy_skip, hy_norm_g, hg_lb_logits, hg_norm_g, out_w, pool_w, pool_b, pool_scale, ffn_up_w, ffn_conv_w, ffn_conv_b, ffn_down_w, final_norm_g):
    batch, _, d = x.shape
    depth = mod_w.shape[0]
    hy_d = hy_norm_g.shape[1]
    hg_d = hg_norm_g.shape[1]
    hg_q0 = 3 * hy_d
    hg_i0 = hg_q0 + hg_d
    hg_heads = hg_d // HG_EXPAND
    ffn_up_bf, ffn_down_bf = ffn_up_w.astype(BF16), ffn_down_w.astype(BF16)
    outs = []
    for bi in range(batch):
        xs = x[bi]
        cvecs = jnp.zeros((SUBLANES, d), F32).at[0].set(c[bi]).at[1].set(c_ctx)
        for l in range(depth):
            mod = _adaln(cvecs, mod_w, mod_b, l)
            sh1, sc1, g1, sh2, sc2, g2 = [mod[0:1, k * d:(k + 1) * d] for k in range(N_MOD)]
            if l % 2 == 0:
                e = l // 2
                in_w_bf = in_w[e].astype(BF16)
                proj = _norm_matmul(xs, norm_mix_g[l], sc1, sh1, in_w_bf, in_b[e])
                ctx_proj = _norm_matmul(ctx[bi], norm_mix_g[l], mod[1:2, d:2 * d], mod[1:2, 0:d],
                                        in_w_bf[:, hg_i0:hg_i0 + 3 * hg_d], in_b[e, hg_i0:hg_i0 + 3 * hg_d])
                filt = (hy_w1[e], hy_b1[e], hy_w2[e], hy_b2[e], hy_w3[e], hy_b3[e], hy_freq[e], hy_w4[e])
                y_hy = _hyena_mixer(proj, hy_conv_w[e], hy_conv_b[e], filt, hy_skip[e], hy_norm_g[e], hy_d)
                y_hg = _hgrn_mixer(proj, hg_q0, ctx_proj, hg_lb_logits, e, hg_norm_g[e], hg_heads)
                xs = _outproj(y_hy, y_hg, out_w[e].astype(BF16), xs, g1)
            else:
                od = l // 2
                xs, h = _pool_layer(xs, norm_mix_g[l], sc1, sh1, pool_w[od], pool_b[od], pool_scale[od], g1,
                                    norm_ffn_g[l], sc2, sh2)
            if l % 2 == 0:
                h = _norm_mod(xs, norm_ffn_g[l], sc2, sh2)
            xs = _conv_ffn(h, xs, g2, ffn_up_bf, ffn_down_bf, l, ffn_conv_w[l], ffn_conv_b[l],
                           final_norm_g if l == depth - 1 else None, GRID_W)
        outs.append(xs[None])
    return outs[0] if batch == 1 else jnp.concatenate(outs, axis=0)
```

```python
import functools
import math

import numpy as np
import jax
import jax.numpy as jnp
from jax import lax
from jax.experimental import pallas as pl
from jax.experimental.pallas import tpu as pltpu

F32 = jnp.float32
BF16 = jnp.bfloat16
U32 = jnp.uint32
EPS = 1e-6

LANES = 128
SUBLANES = 8
VMEM_LIMIT_BYTES = 56 * 1024 * 1024

GRID_W = 64
N_MOD = 6
HY_HEADS = 8
HY_ORDER = 2
HY_EMB = 33
HY_DECAY_TARGET = 1e-2
HY_FAST_PCT = 0.3
HY_SLOW_PCT = 1.5
HG_EXPAND = 128
HG_CHUNK = 64
POOL_WINDOWS = (2, 4, 8, 16)
POOL_HALO = 8
FFT_B = 128
FFN_ROWS = 1024


def _params(*sem):
    return pltpu.CompilerParams(dimension_semantics=sem, vmem_limit_bytes=VMEM_LIMIT_BYTES)


def _sds(shape, dtype):
    return jax.ShapeDtypeStruct(shape, dtype)


def _dot(a, b):
    return jnp.dot(a, b, preferred_element_type=F32)


def _adaln_kernel(c_ref, w_ref, b_ref, o_ref):
    c = c_ref[...]
    s = (c * jax.nn.sigmoid(c)).astype(BF16)
    o_ref[...] = _dot(s, w_ref[...].astype(BF16)) + b_ref[...]


def _adaln(cvecs, mod_w, mod_b, layer, tn=1024):
    _, d, n = mod_w.shape
    tn = min(tn, n)
    return pl.pallas_call(
        _adaln_kernel,
        grid=(n // tn,),
        in_specs=[
            pl.BlockSpec((SUBLANES, d), lambda j: (0, 0)),
            pl.BlockSpec((None, d, tn), lambda j: (layer, 0, j)),
            pl.BlockSpec((None, 1, tn), lambda j: (layer, 0, j)),
        ],
        out_specs=pl.BlockSpec((SUBLANES, tn), lambda j: (0, j)),
        out_shape=_sds((SUBLANES, n), F32),
        compiler_params=_params("arbitrary"),
        name="adaln",
    )(cvecs, mod_w, mod_b.reshape(mod_b.shape[0], 1, n))


def _rms_mod(x, g, sc, sh):
    y = x * lax.rsqrt(jnp.mean(x * x, axis=-1, keepdims=True) + EPS) * g
    return y * (1.0 + sc) + sh


def _norm_mod_kernel(x_ref, g_ref, sc_ref, sh_ref, o_ref):
    o_ref[...] = _rms_mod(x_ref[...], g_ref[...], sc_ref[...], sh_ref[...]).astype(o_ref.dtype)


def _norm_mod(x, g, sc, sh, tm=1024):
    m, d = x.shape
    tm = min(tm, m)
    row = pl.BlockSpec((1, d), lambda i: (0, 0))
    return pl.pallas_call(
        _norm_mod_kernel,
        grid=(m // tm,),
        in_specs=[pl.BlockSpec((tm, d), lambda i: (i, 0)), row, row, row],
        out_specs=pl.BlockSpec((tm, d), lambda i: (i, 0)),
        out_shape=_sds((m, d), BF16),
        compiler_params=_params("parallel"),
        name="norm_mod",
    )(x, g.reshape(1, d), sc, sh)


def _norm_matmul_kernel(x_ref, g_ref, sc_ref, sh_ref, w_ref, b_ref, o_ref, h_ref, *, lookahead):
    i, j = pl.program_id(0), pl.program_id(1)
    last = pl.num_programs(1) - 1
    cur = i % 2

    def norm_into(slot):
        h_ref[slot] = _rms_mod(x_ref[...], g_ref[...], sc_ref[...], sh_ref[...]).astype(h_ref.dtype)

    def matmul():
        o_ref[...] = _dot(h_ref[cur], w_ref[...]) + b_ref[...]

    if not lookahead:
        @pl.when(j == 0)
        def _():
            norm_into(cur)

        matmul()
        return

    @pl.when(j < last)
    def _():
        @pl.when(jnp.logical_and(i == 0, j == 0))
        def _():
            norm_into(0)

        matmul()

    @pl.when(j == last)
    def _():
        norm_into(1 - cur)
        matmul()


def _norm_matmul(x, g, sc, sh, w, b, tm=1024, tn=1024):
    m, k = x.shape
    n = w.shape[1]
    tm, tn = min(tm, m), min(tn, n)
    ni, nj = m // tm, n // tn
    lookahead = nj > 1
    row = pl.BlockSpec((1, k), lambda i, j: (0, 0))
    x_block = ((lambda i, j: (jnp.minimum(jnp.where(j == nj - 1, i + 1, i), ni - 1), 0)) if lookahead
               else (lambda i, j: (i, 0)))
    return pl.pallas_call(
        functools.partial(_norm_matmul_kernel, lookahead=lookahead),
        grid=(ni, nj),
        in_specs=[
            pl.BlockSpec((tm, k), x_block), row, row, row,
            pl.BlockSpec((k, tn), lambda i, j: (0, j)),
            pl.BlockSpec((1, tn), lambda i, j: (0, j)),
        ],
        out_specs=pl.BlockSpec((tm, tn), lambda i, j: (i, j)),
        out_shape=_sds((m, n), F32),
        scratch_shapes=[pltpu.VMEM((2, tm, k), BF16)],
        compiler_params=_params("arbitrary", "arbitrary"),
        name="norm_matmul",
    )(x, g.reshape(1, k), sc, sh, w, b.reshape(1, n))


def _dwconv1d_kernel(xm_ref, xp_ref, xn_ref, w_ref, b_ref, o_ref):
    i = pl.program_id(0)
    x = xm_ref[...]
    tm = x.shape[0]
    prev_row = jnp.where(i > 0, xp_ref[SUBLANES - 1:SUBLANES, :], 0.0)
    next_row = jnp.where(i < pl.num_programs(0) - 1, xn_ref[0:1, :], 0.0)
    row = lax.broadcasted_iota(jnp.int32, x.shape, 0)
    x_m1 = jnp.where(row == 0, prev_row, pltpu.roll(x, 1, 0))
    x_p1 = jnp.where(row == tm - 1, next_row, pltpu.roll(x, tm - 1, 0))
    o_ref[...] = x_m1 * w_ref[0:1, :] + x * w_ref[1:2, :] + x_p1 * w_ref[2:3, :] + b_ref[...]


def _dwconv1d(proj, w, b, parts, c, tm=1024):
    l = proj.shape[0]
    tm = min(tm, l)
    r = tm // SUBLANES
    last = l // SUBLANES - 1
    return pl.pallas_call(
        _dwconv1d_kernel,
        grid=(l // tm, parts),
        in_specs=[
            pl.BlockSpec((tm, c), lambda i, j: (i, j)),
            pl.BlockSpec((SUBLANES, c), lambda i, j: (jnp.maximum(i * r - 1, 0), j)),
            pl.BlockSpec((SUBLANES, c), lambda i, j: (jnp.minimum((i + 1) * r, last), j)),
            pl.BlockSpec((3, c), lambda i, j: (0, j)),
            pl.BlockSpec((1, c), lambda i, j: (0, j)),
        ],
        out_specs=pl.BlockSpec((None, tm, c), lambda i, j: (j, i, 0)),
        out_shape=_sds((parts, l, c), F32),
        compiler_params=_params("parallel", "parallel"),
        name="hyena_dwconv1d",
    )(proj, proj, proj, w, b.reshape(1, parts * c))


def _hyena_pos_features(l):
    t = jnp.linspace(0.0, 1.0, l, dtype=F32)[:, None]
    bands = (HY_EMB - 1) // 2
    f = jnp.linspace(1e-4, bands - 1, bands, dtype=F32)[None, :]
    w = (2.0 * math.pi / l) * jnp.arange(l, dtype=F32)[:, None]
    return jnp.concatenate([t, jnp.cos(f * w), -jnp.sin(f * w)], axis=-1)


def _stage2_matrices():
    b = FFT_B
    angb = 2.0 * np.pi * np.outer(np.arange(b), np.arange(b)) / b
    cb, sb = np.cos(angb), np.sin(angb)
    m_fwd = np.block([[cb, sb], [-sb, cb]])
    m_inv = np.block([[cb, -sb], [sb, cb]])
    pairs = np.arange(2 * b)
    pairs = (pairs % 2) * b + pairs // 2
    as_bf16 = lambda m: jnp.asarray(m, dtype=F32).astype(BF16)
    return as_bf16(m_fwd[:, pairs]), as_bf16(m_inv[pairs, :])


def _stage1_matrices(l):
    b = FFT_B
    a = l // b
    na = 2 * a
    n = 2 * l
    idx = jnp.arange(na, dtype=jnp.int32)
    alpha = ((idx[:, None] * idx[None, :]) % na).astype(F32) * (2.0 * math.pi / na)
    beta = ((jnp.arange(b, dtype=jnp.int32)[:, None] * idx[None, :]) % n).astype(F32) * (2.0 * math.pi / n)
    ca, sa, cb, sb = jnp.cos(alpha), jnp.sin(alpha), jnp.cos(beta), jnp.sin(beta)
    cb1, sb1 = cb[:, :, None], sb[:, :, None]
    re1, im1 = ca[None] * cb1 - sa[None] * sb1, -(sa[None] * cb1 + ca[None] * sb1)
    g1 = jnp.stack([re1, im1], axis=2).reshape(b, 2 * na, na).astype(BF16)
    ca3, sa3 = ca[None, :a, :] * (1.0 / n), sa[None, :a, :] * (1.0 / n)
    cb3, sb3 = cb[:, None, :], sb[:, None, :]
    re3, im3 = ca3 * cb3 - sa3 * sb3, -(sa3 * cb3 + ca3 * sb3)
    g3 = jnp.stack([re3, im3], axis=3).reshape(b, a, 2 * na).astype(BF16)
    return g1, g3


def _cmul(ar, ai, br, bi):
    return ar * br - ai * bi, ar * bi + ai * br


def _pack_pairs(y):
    return pltpu.bitcast(y.astype(BF16), U32)


def _unpack_pairs(w):
    return pltpu.bitcast(w, BF16)


def _strided_rows(chunk_refs, s, rows, stride):
    flat = [r.reshape(rows * stride, LANES) for r in chunk_refs]
    return jnp.concatenate([r[pl.ds(s, rows, stride=stride), :] for r in flat], axis=-1)


def _fft_stage1_kernel(*refs, segs, nq, strided):
    f_ref, x_refs, o_ref = refs[0], refs[1:1 + nq], refs[1 + nq]
    ka = f_ref.shape[2]
    for s in range(segs):
        x = _strided_rows(x_refs, s, ka, segs) if strided else x_refs[0][s]
        o_ref[s] = _pack_pairs(_dot(f_ref[s], x.astype(BF16)))


def _fft_stage1(x4, part, g1, strided, cb=512):
    if strided:
        _, ka, b, c = x4.shape
    else:
        _, b, ka, c = x4.shape
    rows = g1.shape[1]
    segs = SUBLANES
    cb = min(cb, c)
    if strided:
        nq = cb // LANES
        x_specs = [pl.BlockSpec((None, ka, segs, LANES), functools.partial(lambda j, jc, q: (part, 0, j, jc * nq + q), q=q))
                   for q in range(nq)]
    else:
        nq = 1
        x_specs = [pl.BlockSpec((None, segs, ka, cb), lambda j, jc: (part, j, 0, jc))]
    return pl.pallas_call(
        functools.partial(_fft_stage1_kernel, segs=segs, nq=nq, strided=strided),
        grid=(b // segs, c // cb),
        in_specs=[pl.BlockSpec((segs, rows, ka), lambda j, jc: (j, 0, 0))] + x_specs,
        out_specs=pl.BlockSpec((segs, rows // 2, cb), lambda j, jc: (j, 0, jc)),
        out_shape=_sds((b, rows // 2, c), U32),
        compiler_params=_params("parallel", "parallel"),
        name="fft_stage1",
    )(g1, *([x4] * nq))


def _fft_filter_stage1_kernel(z_ref, w1_ref, b1_ref, w2_ref, b2_ref, w3_ref, b3_ref, fr_ref, w4_ref, dl_ref, f_ref,
                              o_ref, h_ref, *, seq, segs):
    a_n = z_ref.shape[1]
    inv = 1.0 / (seq - 1)

    @pl.when(pl.program_id(1) == 0)
    def _():
        for s in range(segs):
            h = jnp.sin(fr_ref[0:1, :] * (_dot(z_ref[s], w1_ref[...]) + b1_ref[...]))
            h = jnp.sin(fr_ref[1:2, :] * (_dot(h.astype(BF16), w2_ref[...]) + b2_ref[...]))
            h = jnp.sin(fr_ref[2:3, :] * (_dot(h.astype(BF16), w3_ref[...]) + b3_ref[...]))
            h_ref[s] = h.astype(BF16)

    for s in range(segs):
        b = pl.program_id(0) * segs + s
        n = (lax.broadcasted_iota(jnp.int32, (a_n, 1), 0) * FFT_B + b).astype(F32)
        e_fwd = jnp.exp(-(n * inv) * dl_ref[...])
        e_rev = jnp.where(n == 0.0, 0.0, jnp.exp(-((seq - n) * inv) * dl_ref[...]))
        h = h_ref[s]
        for o in range(HY_ORDER):
            k = jnp.concatenate([_dot(h, w4_ref[2 * o]) * e_fwd, _dot(h, w4_ref[2 * o + 1]) * e_rev], axis=0)
            o_ref[o, s] = _pack_pairs(_dot(f_ref[s], k.astype(BF16)))


def _block_diag2(w):
    z = jnp.zeros_like(w)
    return jnp.concatenate([jnp.concatenate([w, z], axis=1), jnp.concatenate([z, w], axis=1)], axis=0)


def _fft_filter_stage1(l, c, w1, b1, w2, b2, w3, b3, freq, w4, g1, cb=256):
    fw = w1.shape[1]
    emb = 64
    a = l // FFT_B
    rows = g1.shape[1]
    segs = SUBLANES
    cb = min(cb, c)
    z = _hyena_pos_features(l)
    z_rev = jnp.concatenate([z[:1], z[:0:-1]], axis=0)
    pad = lambda t: jnp.pad(t, ((0, 0), (0, emb - HY_EMB)))
    z2 = jnp.concatenate([pad(z), pad(z_rev)], axis=1).astype(BF16)
    z2 = z2.reshape(a, FFT_B, 2 * emb).transpose(1, 0, 2)
    both = lambda v: jnp.concatenate([v, v]).reshape(1, 2 * fw)
    w1d = _block_diag2(jnp.pad(w1, ((0, emb - HY_EMB), (0, 0)))).astype(BF16)
    w4r = w4.reshape(fw, HY_ORDER * 2, c).transpose(1, 0, 2)
    zeros = jnp.zeros_like(w4r)
    fwd_dir = (jnp.arange(HY_ORDER * 2) % 2 == 0)[:, None, None]
    w4d = jnp.concatenate([jnp.where(fwd_dir, w4r, zeros), jnp.where(fwd_dir, zeros, w4r)], axis=1).astype(BF16)
    max_decay = math.log(HY_DECAY_TARGET) / HY_FAST_PCT
    min_decay = math.log(HY_DECAY_TARGET) / HY_SLOW_PCT
    deltas = jnp.abs(jnp.linspace(min_decay, max_decay, c, dtype=F32)).reshape(1, c)
    full = lambda shape: pl.BlockSpec(shape, lambda j, jc: (0,) * len(shape))
    return pl.pallas_call(
        functools.partial(_fft_filter_stage1_kernel, seq=l, segs=segs),
        grid=(FFT_B // segs, c // cb),
        in_specs=[
            pl.BlockSpec((segs, a, 2 * emb), lambda j, jc: (j, 0, 0)),
            full((2 * emb, 2 * fw)), full((1, 2 * fw)), full((2 * fw, 2 * fw)), full((1, 2 * fw)),
            full((2 * fw, 2 * fw)), full((1, 2 * fw)), full((3, 2 * fw)),
            pl.BlockSpec((HY_ORDER * 2, 2 * fw, cb), lambda j, jc: (0, 0, jc)),
            pl.BlockSpec((1, cb), lambda j, jc: (0, jc)),
            pl.BlockSpec((segs, rows, 2 * a), lambda j, jc: (j, 0, 0)),
        ],
        out_specs=pl.BlockSpec((HY_ORDER, segs, rows // 2, cb), lambda j, jc: (0, j, 0, jc)),
        out_shape=_sds((HY_ORDER, FFT_B, rows // 2, c), U32),
        scratch_shapes=[pltpu.VMEM((segs, a, 2 * fw), BF16)],
        compiler_params=_params("parallel", "arbitrary"),
        name="fft_filter_stage1",
    )(z2, w1d, both(b1), _block_diag2(w2).astype(BF16), both(b2), _block_diag2(w3).astype(BF16), both(b3),
      jnp.concatenate([freq, freq], axis=1), w4d, deltas, g1)


def _fft_mid_kernel(*refs, kq, nc):
    mf_ref, mi_ref = refs[0], refs[1]
    y_refs, k_refs = refs[2:2 + nc], refs[2 + nc:2 + 2 * nc]
    o_ref = refs[2 + 2 * nc]
    b = mf_ref.shape[0] // 2
    for q in range(kq):
        x = _dot(mf_ref[...], _unpack_pairs(_strided_rows(y_refs, q, b, kq)))
        k = _dot(mf_ref[...], _unpack_pairs(_strided_rows(k_refs, q, b, kq)))
        pr, pi = _cmul(x[:b], x[b:], k[:b], k[b:])
        o_ref[q] = _pack_pairs(_dot(mi_ref[...], jnp.concatenate([pr, pi], axis=0).astype(BF16)))


def _fft_mid(y3, yk4, order, m_fwd, m_inv):
    b, na, c = y3.shape
    kq = SUBLANES
    nc = c // LANES
    mat = pl.BlockSpec((2 * b, 2 * b), lambda i: (0, 0))
    y_specs = [pl.BlockSpec((b, kq, LANES), functools.partial(lambda i, q: (0, i, q), q=q)) for q in range(nc)]
    k_specs = [pl.BlockSpec((None, b, kq, LANES), functools.partial(lambda i, q: (order, 0, i, q), q=q))
               for q in range(nc)]
    return pl.pallas_call(
        functools.partial(_fft_mid_kernel, kq=kq, nc=nc),
        grid=(na // kq,),
        in_specs=[mat, mat] + y_specs + k_specs,
        out_specs=pl.BlockSpec((kq, b, c), lambda i: (i, 0, 0)),
        out_shape=_sds((na, b, c), U32),
        compiler_params=_params("parallel"),
        name="fft_mid",
    )(m_fwd, m_inv, *([y3] * nc), *([yk4] * nc))


def _fft_stage3_kernel(*refs, segs, nq, u_strided, head_dim):
    f_ref, v_refs = refs[0], refs[1:1 + nq]
    nu = nq if u_strided else 1
    u_refs, g_refs = refs[1 + nq:1 + nq + nu], refs[1 + nq + nu:1 + 2 * nq + nu]
    sk_ref, ng_ref, o_ref = refs[1 + 2 * nq + nu:]
    a = f_ref.shape[1]
    rows = f_ref.shape[2] // 2
    c = o_ref.shape[-1]
    for s in range(segs):
        y = _dot(f_ref[s], _unpack_pairs(_strided_rows(v_refs, s, rows, segs)))
        u = _strided_rows(u_refs, s, a, segs) if u_strided else u_refs[0][s]
        z = _strided_rows(g_refs, s, a, segs) * (y + u * sk_ref[...])
        if head_dim:
            heads = [z[:, h:h + head_dim] for h in range(0, c, head_dim)]
            heads = [zh * lax.rsqrt(jnp.mean(zh * zh, axis=-1, keepdims=True) + EPS) for zh in heads]
            z = jnp.concatenate(heads, axis=-1) * ng_ref[...]
        o_ref[s] = z


def _fft_stage3(v3, f3, u4, u_part, u_strided, g4, g_part, skip, norm_g, head_dim, cb=512):
    a = f3.shape[1]
    rows, b, c = v3.shape
    segs = SUBLANES
    cb = min(cb, c)
    nq = cb // LANES
    chunk = lambda shape, idx: [pl.BlockSpec(shape, functools.partial(idx, q=q)) for q in range(nq)]
    v_specs = chunk((rows, segs, LANES), lambda j, jc, q: (0, j, jc * nq + q))
    nat = lambda part: chunk((None, a, segs, LANES), lambda j, jc, q: (part, 0, j, jc * nq + q))
    u_specs = nat(u_part) if u_strided else [pl.BlockSpec((None, segs, a, cb), lambda j, jc: (u_part, j, 0, jc))]
    vec = pl.BlockSpec((1, cb), lambda j, jc: (0, jc))
    return pl.pallas_call(
        functools.partial(_fft_stage3_kernel, segs=segs, nq=nq, u_strided=u_strided, head_dim=head_dim),
        grid=(b // segs, c // cb),
        in_specs=[pl.BlockSpec((segs, a, 2 * rows), lambda j, jc: (j, 0, 0))] + v_specs + u_specs + nat(g_part)
        + [vec, vec],
        out_specs=pl.BlockSpec((segs, a, cb), lambda j, jc: (j, 0, jc)),
        out_shape=_sds((b, a, c), F32),
        compiler_params=_params("parallel", "parallel"),
        name="fft_stage3",
    )(f3, *([v3] * nq), *([u4] * (nq if u_strided else 1)), *([g4] * nq), skip.reshape(1, c), norm_g.reshape(1, c))


def _hyena_mixer(proj, conv_w, conv_b, filt, skip, norm_g, c):
    l = proj.shape[0]
    a = l // FFT_B
    hyc4 = _dwconv1d(proj, conv_w, conv_b, 3, c).reshape(3, a, FFT_B, c)
    m_fwd, m_inv = _stage2_matrices()
    g1, f3 = _stage1_matrices(l)
    yk = _fft_filter_stage1(l, c, *filt, g1)
    z4, part, strided = hyc4, 0, True
    for o in range(HY_ORDER):
        y3 = _fft_stage1(z4, part, g1, strided)
        v3 = _fft_mid(y3, yk, o, m_fwd, m_inv)
        last = o == HY_ORDER - 1
        z = _fft_stage3(v3, f3, z4, part, strided, hyc4, 1 + o, skip[o], norm_g, c // HY_HEADS if last else 0)
        z4, part, strided = z[None], 0, False
    return z


def _lower_bound(lg_ref, slot):
    lg = lg_ref[...]
    e = jnp.exp(lg - jnp.max(lg, axis=0, keepdims=True))
    sm = e / jnp.sum(e, axis=0, keepdims=True)
    return jnp.sum(sm[:slot + 1], axis=0, keepdims=True)


def _prefix_sum_rows(x):
    t = x.shape[0]
    row = lax.broadcasted_iota(jnp.int32, x.shape, 0)
    s = 1
    while s < t:
        x = x + jnp.where(row >= s, pltpu.roll(x, s, 0), 0.0)
        s *= 2
    return x


def _gla_kernel(*refs, slot, reverse, nchunk, chunk, hpb, with_out, combine):
    it = iter(refs)
    i_ref, f_ref, lg_ref, s0_ref = next(it), next(it), next(it), next(it)
    q_ref = next(it) if with_out else None
    ofw_ref, g_ref, ng_ref = (next(it), next(it), next(it)) if combine else (None, None, None)
    o_ref = next(it) if with_out else None
    sf_ref, st_ref = next(it), next(it)
    ke = HG_EXPAND

    @pl.when(pl.program_id(1) == 0)
    def _():
        st_ref[...] = s0_ref[...]

    lb = _lower_bound(lg_ref, slot)
    row = lax.broadcasted_iota(jnp.int32, (chunk, chunk), 0)
    col = lax.broadcasted_iota(jnp.int32, (chunk, chunk), 1)
    keep = (col >= row) if reverse else (col <= row)
    nt = (((1,), (1,)), ((), ()))
    tn = (((0,), (0,)), ((), ()))
    st = [st_ref[h] for h in range(hpb)]
    order = range(nchunk - 1, -1, -1) if reverse else range(nchunk)
    for ci in order:
        rows = slice(ci * chunk, (ci + 1) * chunk)
        f = lb + (1.0 - lb) * jax.nn.sigmoid(f_ref[rows, :])
        logf = jnp.log(f)
        key = 1.0 - f
        b = _prefix_sum_rows(logf)
        total = b[chunk - 1:chunk, :]
        if reverse:
            b = total - b + logf
        v = i_ref[rows, :].astype(BF16)
        k_state = (key * jnp.exp(total - b)).astype(BF16)
        decay = jnp.exp(total)
        if with_out:
            q = q_ref[rows, :]
            q_in = (q * jax.nn.sigmoid(q) * jnp.exp(b)).astype(BF16)
            k_intra = (key * jnp.exp(-b)).astype(BF16)
        outs = []
        for h in range(hpb):
            hl = slice(h * ke, (h + 1) * ke)
            if with_out:
                att = lax.dot_general(q_in[:, hl], k_intra[:, hl], nt, preferred_element_type=F32)
                att = jnp.where(keep, att, 0.0).astype(BF16)
                o = _dot(att, v[:, hl]) + lax.dot_general(q_in[:, hl], st[h].astype(BF16), nt,
                                                           preferred_element_type=F32)
                if combine:
                    o = o + ofw_ref[rows, hl]
                    o = o * lax.rsqrt(jnp.mean(o * o, axis=-1, keepdims=True) + EPS)
                outs.append(o)
            st[h] = st[h] * decay[:, hl] + lax.dot_general(v[:, hl], k_state[:, hl], tn, preferred_element_type=F32)
        if with_out:
            o = jnp.concatenate(outs, axis=-1)
            if combine:
                g = g_ref[rows, :]
                o = o * ng_ref[...] * (g * jax.nn.sigmoid(g))
            o_ref[rows, :] = o.astype(o_ref.dtype)
    for h in range(hpb):
        st_ref[h] = st[h]

    @pl.when(pl.program_id(1) == pl.num_programs(1) - 1)
    def _():
        for h in range(hpb):
            sf_ref[h] = st[h]


def _gla(src, cols, lb_logits, slot, s0, reverse, with_out, fwd_out=None, norm_g=None, out_dtype=F32, tb=512, hpb=8):
    l = src.shape[0]
    heads = s0.shape[0]
    ke = HG_EXPAND
    tb = min(tb, l)
    nblk = l // tb
    while heads % hpb or any(off % hpb for off in cols.values()):
        hpb //= 2
    wd = hpb * ke
    combine = fwd_out is not None
    blk = (lambda j: nblk - 1 - j) if reverse else (lambda j: j)
    col = lambda name: pl.BlockSpec((tb, wd), lambda h, j: (blk(j), cols[name] // hpb + h))
    head_rows = pl.BlockSpec((tb, wd), lambda h, j: (blk(j), h))
    state = pl.BlockSpec((hpb, ke, ke), lambda h, j: (h, 0, 0))
    in_specs = [col("i"), col("f"), pl.BlockSpec((lb_logits.shape[0], wd), lambda h, j: (0, h)), state]
    args = [src, src, lb_logits, s0]
    out_specs, out_shape = [], []
    if with_out:
        in_specs.append(col("q"))
        args.append(src)
        out_specs.append(head_rows)
        out_shape.append(_sds((l, heads * ke), out_dtype))
    if combine:
        in_specs += [head_rows, col("g"), pl.BlockSpec((1, wd), lambda h, j: (0, h))]
        args += [fwd_out, src, norm_g.reshape(1, heads * ke)]
    out_specs.append(state)
    out_shape.append(_sds((heads, ke, ke), F32))
    res = pl.pallas_call(
        functools.partial(_gla_kernel, slot=slot, reverse=reverse, nchunk=tb // HG_CHUNK, chunk=HG_CHUNK, hpb=hpb,
                          with_out=with_out, combine=combine),
        grid=(heads // hpb, nblk),
        in_specs=in_specs,
        out_specs=out_specs,
        out_shape=out_shape,
        scratch_shapes=[pltpu.VMEM((hpb, ke, ke), F32)],
        compiler_params=_params("parallel", "arbitrary"),
        name="hgrn_gla",
    )(*args)
    return res if with_out else (None, res[0])


def _hgrn_mixer(proj, q0, ctx_proj, lb_logits, slot, norm_g, heads):
    qb = q0 // HG_EXPAND
    zeros = jnp.zeros((heads, HG_EXPAND, HG_EXPAND), F32)
    _, s0_fw = _gla(ctx_proj, {"i": 0, "f": heads}, lb_logits, slot, zeros, False, False)
    _, s0_bw = _gla(ctx_proj, {"i": 0, "f": 2 * heads}, lb_logits, slot, zeros, True, False)
    cols = {"q": qb, "i": qb + heads, "f": qb + 2 * heads, "g": qb + 4 * heads}
    o_fw, _ = _gla(proj, cols, lb_logits, slot, s0_fw, False, True)
    cols["f"] = qb + 3 * heads
    y, _ = _gla(proj, cols, lb_logits, slot, s0_bw, True, True, fwd_out=o_fw, norm_g=norm_g, out_dtype=BF16)
    return y


def _outproj_kernel(*refs, na, nc):
    a_refs = refs[:nc]
    b_ref, w_ref, x_ref, g_ref, o_ref = refs[nc:]
    rows = a_refs[0].shape[0]
    ya =jnp.concatenate([_strided_rows(a_refs, s, rows, na) for s in range(na)], axis=0).astype(BF16)
    y = _dot(jnp.concatenate([ya, b_ref[...]], axis=1), w_ref[...])
    o_ref[...] = x_ref[...] + g_ref[...] * y


def _outproj(ya3, yb, w, x, gate, tn=1024):
    m, d = x.shape
    b, a, k1 = ya3.shape
    k2 = yb.shape[1]
    na = min(SUBLANES, a)
    tm = na * b
    tn = min(tn, d)
    nc = k1 // LANES
    a_specs = [pl.BlockSpec((b, na, LANES), functools.partial(lambda j, i, q: (0, i, q), q=q)) for q in range(nc)]
    return pl.pallas_call(
        functools.partial(_outproj_kernel, na=na, nc=nc),
        grid=(d // tn, m // tm),
        in_specs=a_specs + [
            pl.BlockSpec((tm, k2), lambda j, i: (i, 0)),
            pl.BlockSpec((k1 + k2, tn), lambda j, i: (0, j)),
            pl.BlockSpec((tm, tn), lambda j, i: (i, j)),
            pl.BlockSpec((1, tn), lambda j, i: (0, j)),
        ],
        out_specs=pl.BlockSpec((tm, tn), lambda j, i: (i, j)),
        out_shape=_sds((m, d), F32),
        compiler_params=_params("parallel", "parallel"),
        name="outproj_residual",
    )(*([ya3] * nc), yb, w, x, gate)


def _ffn_kernel(h_hbm, wa_ref, wu_ref, cw_ref, cb_ref, wd_ref, x_hbm, g_ref, fg_ref, o_ref,
                hext_ref, x_ref, act_ref, sem, *, gw, final):
    i, j = pl.program_id(0), pl.program_id(1)
    ni, nj = pl.num_programs(0), pl.num_programs(1) - 1
    tm = o_ref.shape[0]
    n = tm + 2 * gw
    row0 = i * tm

    def h_copy(src_row, dst_row, rows, k):
        return pltpu.make_async_copy(h_hbm.at[pl.ds(src_row, rows)], hext_ref.at[pl.ds(dst_row, rows)], sem.at[k])

    x_copy = pltpu.make_async_copy(x_hbm.at[pl.ds(row0, tm)], x_ref, sem.at[3])

    def h_block(blk, start):
        r0 = blk * tm
        act = (lambda c: c.start()) if start else (lambda c: c.wait())
        act(h_copy(r0, gw, tm, 0))

        @pl.when(blk > 0)
        def _():
            act(h_copy(r0 - gw, 0, gw, 1))

        @pl.when(blk < ni - 1)
        def _():
            act(h_copy(r0 + tm, gw + tm, gw, 2))

        if start:
            @pl.when(blk == 0)
            def _():
                hext_ref[0:gw, :] = jnp.zeros((gw, hext_ref.shape[1]), hext_ref.dtype)

            @pl.when(blk == ni - 1)
            def _():
                hext_ref[gw + tm:n, :] = jnp.zeros((gw, hext_ref.shape[1]), hext_ref.dtype)

    def up_part():
        a = _dot(hext_ref[...], wa_ref[...])
        u = _dot(hext_ref[gw:gw + tm, :], wu_ref[...])
        col = lax.broadcasted_iota(jnp.int32, a.shape, 0) % gw
        a_m1 = jnp.where(col == 0, 0.0, pltpu.roll(a, 1, 0))
        a_p1 = jnp.where(col == gw - 1, 0.0, pltpu.roll(a, n - 1, 0))
        conv = cb_ref[...]
        for di in range(3):
            rows = slice(di * gw, di * gw + tm)
            conv = conv + (a_m1[rows] * cw_ref[3 * di:3 * di + 1, :] + a[rows] * cw_ref[3 * di + 1:3 * di + 2, :]
                           + a_p1[rows] * cw_ref[3 * di + 2:3 * di + 3, :])
        gelu = 0.5 * conv * (1.0 + lax.erf(conv * (1.0 / math.sqrt(2.0))))
        act_ref[j % 2] = (gelu * u).astype(BF16)

    def down_part():
        o_ref[...] += _dot(act_ref[(j + 1) % 2], wd_ref[...])

    @pl.when(j == 0)
    def _():
        @pl.when(i == 0)
        def _():
            h_block(i, start=True)

        x_copy.start()
        o_ref[...] = jnp.zeros_like(o_ref)
        h_block(i, start=False)
        up_part()

    @pl.when(jnp.logical_and(j > 0, j < nj))
    def _():
        down_part()
        up_part()

    @pl.when(j == nj)
    def _():
        @pl.when(i < ni - 1)
        def _():
            h_block(i + 1, start=True)

        down_part()
        x_copy.wait()
        y = x_ref[...] + g_ref[...] * o_ref[...]
        if final:
            y = y * lax.rsqrt(jnp.mean(y * y, axis=-1, keepdims=True) + EPS) * fg_ref[...]
        o_ref[...] = y


def _conv_ffn(h, x, gate, w_up, w_down, layer, conv_w, conv_b, final_g, gw, tf=512):
    l, d = x.shape
    dff = w_down.shape[1]
    tm, tf = min(FFN_ROWS, l), min(tf, dff)
    nj = dff // tf
    final = final_g is not None
    fg = (final_g if final else jnp.ones((d,), F32)).reshape(1, d)
    hbm = pl.BlockSpec(memory_space=pl.ANY)
    up = lambda j: jnp.minimum(j, nj - 1)
    down = lambda j: jnp.maximum(j - 1, 0)
    return pl.pallas_call(
        functools.partial(_ffn_kernel, gw=gw, final=final),
        grid=(l // tm, nj + 1),
        in_specs=[
            hbm,
            pl.BlockSpec((None, d, tf), lambda i, j: (layer, 0, up(j))),
            pl.BlockSpec((None, d, tf), lambda i, j: (layer, 0, nj + up(j))),
            pl.BlockSpec((9, tf), lambda i, j: (0, up(j))),
            pl.BlockSpec((1, tf), lambda i, j: (0, up(j))),
            pl.BlockSpec((None, tf, d), lambda i, j: (layer, down(j), 0)),
            hbm,
            pl.BlockSpec((1, d), lambda i, j: (0, 0)),
            pl.BlockSpec((1, d), lambda i, j: (0, 0)),
        ],
        out_specs=pl.BlockSpec((tm, d), lambda i, j: (i, 0)),
        out_shape=_sds((l, d), F32),
        scratch_shapes=[pltpu.VMEM((tm + 2 * gw, d), BF16), pltpu.VMEM((tm, d), F32),
                        pltpu.VMEM((2, tm, tf), BF16), pltpu.SemaphoreType.DMA((4,))],
        compiler_params=_params("arbitrary", "arbitrary"),
        name="conv_ffn",
    )(h, w_up, w_up, conv_w.reshape(9, dff), conv_b.reshape(1, dff), w_down, x, gate, fg)


def _pool_kernel(xm_ref, xp_ref, xn_ref, ng_ref, sc_ref, sh_ref, w_ref, b_ref, ps_ref, g_ref, ng2_ref, sc2_ref,
                 sh2_ref, o_ref, h_ref, *, seq, windows):
    i = pl.program_id(0)
    tm = xm_ref.shape[0]
    pd = w_ref.shape[1]
    norm = lambda x: _rms_mod(x, ng_ref[...], sc_ref[...], sh_ref[...])
    xm = xm_ref[...]
    hm = norm(xm)
    hp = jnp.where(i > 0, norm(xp_ref[...]), 0.0)
    hn = jnp.where(i < pl.num_programs(0) - 1, norm(xn_ref[...]), 0.0)
    t = i * tm + lax.broadcasted_iota(jnp.int32, (tm, 1), 0)
    n = tm + 2 * POOL_HALO
    for gi, win in enumerate(windows):
        cols = slice(gi * pd, (gi + 1) * pd)
        e = jnp.concatenate([hp[:, cols], hm[:, cols], hn[:, cols]], axis=0)
        p = e + pltpu.roll(e, 1, 0)
        w = 2
        while w < win:
            p = pltpu.roll(p, w // 2, 0) + pltpu.roll(p, n - w // 2, 0)
            w *= 2
        count = (jnp.clip(t + win // 2, 0, seq) - jnp.clip(t - win // 2, 0, seq)).astype(F32)
        mean = p[POOL_HALO:POOL_HALO + tm] / count
        y = _dot((mean - hm[:, cols]).astype(BF16), w_ref[gi]) + b_ref[gi]
        o_ref[:, cols] = xm[:, cols] + g_ref[:, cols] * (y * ps_ref[:, cols])
    h_ref[...] = _rms_mod(o_ref[...], ng2_ref[...], sc2_ref[...], sh2_ref[...]).astype(h_ref.dtype)


def _pool_layer(x, norm_g, sc, sh, w, b, scale, gate, norm2_g, sc2, sh2, tm=512):
    l, d = x.shape
    ng, pd, _ = w.shape
    assert max(POOL_WINDOWS) // 2 <= POOL_HALO == SUBLANES
    tm = min(tm, l)
    r = tm // SUBLANES
    last = l // SUBLANES - 1
    row = pl.BlockSpec((1, d), lambda i: (0, 0))
    return pl.pallas_call(
        functools.partial(_pool_kernel, seq=l, windows=POOL_WINDOWS),
        grid=(l // tm,),
        in_specs=[
            pl.BlockSpec((tm, d), lambda i: (i, 0)),
            pl.BlockSpec((SUBLANES, d), lambda i: (jnp.maximum(i * r - 1, 0), 0)),
            pl.BlockSpec((SUBLANES, d), lambda i: (jnp.minimum((i + 1) * r, last), 0)),
            row, row, row,
            pl.BlockSpec((ng, pd, pd), lambda i: (0, 0, 0)),
            pl.BlockSpec((ng, 1, pd), lambda i: (0, 0, 0)),
            row, row, row, row, row,
        ],
        out_specs=[pl.BlockSpec((tm, d), lambda i: (i, 0)), pl.BlockSpec((tm, d), lambda i: (i, 0))],
        out_shape=[_sds((l, d), F32), _sds((l, d), BF16)],
        compiler_params=_params("parallel"),
        name="pool_mixer_residual",
    )(x, x, x, norm_g.reshape(1, d), sc, sh, w.astype(BF16), b.reshape(ng, 1, pd), scale.reshape(1, d), gate,
      norm2_g.reshape(1, d), sc2, sh2)


def kernel(x, c, ctx, c_ctx, norm_mix_g, norm_ffn_g, mod_w, mod_b, in_w, in_b, hy_conv_w, hy_conv_b, hy_w1, hy_b1, hy_w2, hy_b2, hy_w3, hy_b3, hy_freq, hy_w4, hy_skip, hy_norm_g, hg_lb_logits, hg_norm_g, out_w, pool_w, pool_b, pool_scale, ffn_up_w, ffn_conv_w, ffn_conv_b, ffn_down_w, final_norm_g):
    batch, _, d = x.shape
    depth = mod_w.shape[0]
    hy_d = hy_norm_g.shape[1]
    hg_d = hg_norm_g.shape[1]
    hg_q0 = 3 * hy_d
    hg_i0 = hg_q0 + hg_d
    hg_heads = hg_d // HG_EXPAND
    ffn_up_bf, ffn_down_bf = ffn_up_w.astype(BF16), ffn_down_w.astype(BF16)
    outs = []
    for bi in range(batch):
        xs = x[bi]
        cvecs = jnp.zeros((SUBLANES, d), F32).at[0].set(c[bi]).at[1].set(c_ctx)
        for l in range(depth):
            mod = _adaln(cvecs, mod_w, mod_b, l)
            sh1, sc1, g1, sh2, sc2, g2 = [mod[0:1, k * d:(k + 1) * d] for k in range(N_MOD)]
            if l % 2 == 0:
                e = l // 2
                in_w_bf = in_w[e].astype(BF16)
                proj = _norm_matmul(xs, norm_mix_g[l], sc1, sh1, in_w_bf, in_b[e])
                ctx_proj = _norm_matmul(ctx[bi], norm_mix_g[l], mod[1:2, d:2 * d], mod[1:2, 0:d],
                                        in_w_bf[:, hg_i0:hg_i0 + 3 * hg_d], in_b[e, hg_i0:hg_i0 + 3 * hg_d])
                filt = (hy_w1[e], hy_b1[e], hy_w2[e], hy_b2[e], hy_w3[e], hy_b3[e], hy_freq[e], hy_w4[e])
                y_hy = _hyena_mixer(proj, hy_conv_w[e], hy_conv_b[e], filt, hy_skip[e], hy_norm_g[e], hy_d)
                y_hg = _hgrn_mixer(proj, hg_q0, ctx_proj, hg_lb_logits, e, hg_norm_g[e], hg_heads)
                xs = _outproj(y_hy, y_hg, out_w[e].astype(BF16), xs, g1)
            else:
                od = l // 2
                xs, h = _pool_layer(xs, norm_mix_g[l], sc1, sh1, pool_w[od], pool_b[od], pool_scale[od], g1,
                                    norm_ffn_g[l], sc2, sh2)
            if l % 2 == 0:
                h = _norm_mod(xs, norm_ffn_g[l], sc2, sh2)
            xs = _conv_ffn(h, xs, g2, ffn_up_bf, ffn_down_bf, l, ffn_conv_w[l], ffn_conv_b[l],
                           final_norm_g if l == depth - 1 else None, GRID_W)
        outs.append(xs[None])
    return outs[0] if batch == 1 else jnp.concatenate(outs, axis=0)
```

```python
import functools
import math

import numpy as np
import jax
import jax.numpy as jnp
from jax import lax
from jax.experimental import pallas as pl
from jax.experimental.pallas import tpu as pltpu

F32 = jnp.float32
BF16 = jnp.bfloat16
U32 = jnp.uint32
EPS = 1e-6

LANES = 128
SUBLANES = 8
VMEM_LIMIT_BYTES = 56 * 1024 * 1024

GRID_W = 64
N_MOD = 6
HY_HEADS = 8
HY_ORDER = 2
HY_EMB = 33
HY_DECAY_TARGET = 1e-2
HY_FAST_PCT = 0.3
HY_SLOW_PCT = 1.5
HG_EXPAND = 128
HG_CHUNK = 64
POOL_WINDOWS = (2, 4, 8, 16)
POOL_HALO = 8
FFT_B = 128
FFN_ROWS = 1024


def _params(*sem):
    return pltpu.CompilerParams(dimension_semantics=sem, vmem_limit_bytes=VMEM_LIMIT_BYTES)


def _sds(shape, dtype):
    return jax.ShapeDtypeStruct(shape, dtype)


def _dot(a, b):
    return jnp.dot(a, b, preferred_element_type=F32)


def _adaln_kernel(c_ref, w_ref, b_ref, o_ref):
    c = c_ref[...]
    s = (c * jax.nn.sigmoid(c)).astype(BF16)
    o_ref[...] = _dot(s, w_ref[...].astype(BF16)) + b_ref[...]


def _adaln(cvecs, mod_w, mod_b, layer, tn=1024):
    _, d, n = mod_w.shape
    tn = min(tn, n)
    return pl.pallas_call(
        _adaln_kernel,
        grid=(n // tn,),
        in_specs=[
            pl.BlockSpec((SUBLANES, d), lambda j: (0, 0)),
            pl.BlockSpec((None, d, tn), lambda j: (layer, 0, j)),
            pl.BlockSpec((None, 1, tn), lambda j: (layer, 0, j)),
        ],
        out_specs=pl.BlockSpec((SUBLANES, tn), lambda j: (0, j)),
        out_shape=_sds((SUBLANES, n), F32),
        compiler_params=_params("arbitrary"),
        name="adaln",
    )(cvecs, mod_w, mod_b.reshape(mod_b.shape[0], 1, n))


def _rms_mod(x, g, sc, sh):
    y = x * lax.rsqrt(jnp.mean(x * x, axis=-1, keepdims=True) + EPS) * g
    return y * (1.0 + sc) + sh


def _norm_mod_kernel(x_ref, g_ref, sc_ref, sh_ref, o_ref):
    o_ref[...] = _rms_mod(x_ref[...], g_ref[...], sc_ref[...], sh_ref[...]).astype(o_ref.dtype)


def _norm_mod(x, g, sc, sh, tm=1024):
    m, d = x.shape
    tm = min(tm, m)
    row = pl.BlockSpec((1, d), lambda i: (0, 0))
    return pl.pallas_call(
        _norm_mod_kernel,
        grid=(m // tm,),
        in_specs=[pl.BlockSpec((tm, d), lambda i: (i, 0)), row, row, row],
        out_specs=pl.BlockSpec((tm, d), lambda i: (i, 0)),
        out_shape=_sds((m, d), BF16),
        compiler_params=_params("parallel"),
        name="norm_mod",
    )(x, g.reshape(1, d), sc, sh)


def _norm_matmul_kernel(x_ref, g_ref, sc_ref, sh_ref, w_ref, b_ref, o_ref, h_ref, *, lookahead):
    i, j = pl.program_id(0), pl.program_id(1)
    last = pl.num_programs(1) - 1
    cur = i % 2

    def norm_into(slot):
        h_ref[slot] = _rms_mod(x_ref[...], g_ref[...], sc_ref[...], sh_ref[...]).astype(h_ref.dtype)

    def matmul():
        o_ref[...] = _dot(h_ref[cur], w_ref[...]) + b_ref[...]

    if not lookahead:
        @pl.when(j == 0)
        def _():
            norm_into(cur)

        matmul()
        return

    @pl.when(j < last)
    def _():
        @pl.when(jnp.logical_and(i == 0, j == 0))
        def _():
            norm_into(0)

        matmul()

    @pl.when(j == last)
    def _():
        norm_into(1 - cur)
        matmul()


def _norm_matmul(x, g, sc, sh, w, b, tm=1024, tn=1024):
    m, k = x.shape
    n = w.shape[1]
    tm, tn = min(tm, m), min(tn, n)
    ni, nj = m // tm, n // tn
    lookahead = nj > 1
    row = pl.BlockSpec((1, k), lambda i, j: (0, 0))
    x_block = ((lambda i, j: (jnp.minimum(jnp.where(j == nj - 1, i + 1, i), ni - 1), 0)) if lookahead
               else (lambda i, j: (i, 0)))
    return pl.pallas_call(
        functools.partial(_norm_matmul_kernel, lookahead=lookahead),
        grid=(ni, nj),
        in_specs=[
            pl.BlockSpec((tm, k), x_block), row, row, row,
            pl.BlockSpec((k, tn), lambda i, j: (0, j)),
            pl.BlockSpec((1, tn), lambda i, j: (0, j)),
        ],
        out_specs=pl.BlockSpec((tm, tn), lambda i, j: (i, j)),
        out_shape=_sds((m, n), F32),
        scratch_shapes=[pltpu.VMEM((2, tm, k), BF16)],
        compiler_params=_params("arbitrary", "arbitrary"),
        name="norm_matmul",
    )(x, g.reshape(1, k), sc, sh, w, b.reshape(1, n))


def _dwconv1d_kernel(xm_ref, xp_ref, xn_ref, w_ref, b_ref, o_ref):
    i = pl.program_id(0)
    x = xm_ref[...]
    tm = x.shape[0]
    prev_row = jnp.where(i > 0, xp_ref[SUBLANES - 1:SUBLANES, :], 0.0)
    next_row = jnp.where(i < pl.num_programs(0) - 1, xn_ref[0:1, :], 0.0)
    row = lax.broadcasted_iota(jnp.int32, x.shape, 0)
    x_m1 = jnp.where(row == 0, prev_row, pltpu.roll(x, 1, 0))
    x_p1 = jnp.where(row == tm - 1, next_row, pltpu.roll(x, tm - 1, 0))
    o_ref[...] = x_m1 * w_ref[0:1, :] + x * w_ref[1:2, :] + x_p1 * w_ref[2:3, :] + b_ref[...]


def _dwconv1d(proj, w, b, parts, c, tm=1024):
    l = proj.shape[0]
    tm = min(tm, l)
    r = tm // SUBLANES
    last = l // SUBLANES - 1
    return pl.pallas_call(
        _dwconv1d_kernel,
        grid=(l // tm, parts),
        in_specs=[
            pl.BlockSpec((tm, c), lambda i, j: (i, j)),
            pl.BlockSpec((SUBLANES, c), lambda i, j: (jnp.maximum(i * r - 1, 0), j)),
            pl.BlockSpec((SUBLANES, c), lambda i, j: (jnp.minimum((i + 1) * r, last), j)),
            pl.BlockSpec((3, c), lambda i, j: (0, j)),
            pl.BlockSpec((1, c), lambda i, j: (0, j)),
        ],
        out_specs=pl.BlockSpec((None, tm, c), lambda i, j: (j, i, 0)),
        out_shape=_sds((parts, l, c), F32),
        compiler_params=_params("parallel", "parallel"),
        name="hyena_dwconv1d",
    )(proj, proj, proj, w, b.reshape(1, parts * c))


def _hyena_pos_features(l):
    t = jnp.linspace(0.0, 1.0, l, dtype=F32)[:, None]
    bands = (HY_EMB - 1) // 2
    f = jnp.linspace(1e-4, bands - 1, bands, dtype=F32)[None, :]
    w = (2.0 * math.pi / l) * jnp.arange(l, dtype=F32)[:, None]
    return jnp.concatenate([t, jnp.cos(f * w), -jnp.sin(f * w)], axis=-1)


def _stage2_matrices():
    b = FFT_B
    angb = 2.0 * np.pi * np.outer(np.arange(b), np.arange(b)) / b
    cb, sb = np.cos(angb), np.sin(angb)
    m_fwd = np.block([[cb, sb], [-sb, cb]])
    m_inv = np.block([[cb, -sb], [sb, cb]])
    pairs = np.arange(2 * b)
    pairs = (pairs % 2) * b + pairs // 2
    as_bf16 = lambda m: jnp.asarray(m, dtype=F32).astype(BF16)
    return as_bf16(m_fwd[:, pairs]), as_bf16(m_inv[pairs, :])


def _stage1_matrices(l):
    b = FFT_B
    a = l // b
    na = 2 * a
    n = 2 * l
    r = jnp.arange(2 * na, dtype=jnp.int32)
    kap, part = r // 2, r % 2
    ai = jnp.arange(na, dtype=jnp.int32)
    quarter = 2.0 * math.pi / (4 * na)
    alpha = (4 * ((kap[:, None] * ai[None, :]) % na) + part[:, None] * na).astype(F32) * quarter
    beta = ((jnp.arange(b, dtype=jnp.int32)[:, None] * kap[None, :]) % n).astype(F32) * (2.0 * math.pi / n)
    ca, sa, cb, sb = jnp.cos(alpha), jnp.sin(alpha), jnp.cos(beta), jnp.sin(beta)
    g1 = (ca[None] * cb[:, :, None] - sa[None] * sb[:, :, None]).astype(BF16)
    ca3, sa3 = ca[:, :a].T * (1.0 / n), sa[:, :a].T * (1.0 / n)
    g3 = (ca3[None] * cb[:, None, :] - sa3[None] * sb[:, None, :]).astype(BF16)
    return g1, g3


def _cmul(ar, ai, br, bi):
    return ar * br - ai * bi, ar * bi + ai * br


def _pack_pairs(y):
    return pltpu.bitcast(y.astype(BF16), U32)


def _unpack_pairs(w):
    return pltpu.bitcast(w, BF16)


def _strided_rows(chunk_refs, s, rows, stride):
    flat = [r.reshape(rows * stride, LANES) for r in chunk_refs]
    return jnp.concatenate([r[pl.ds(s, rows, stride=stride), :] for r in flat], axis=-1)


def _fft_stage1_kernel(*refs, segs, nq, strided):
    f_ref, x_refs, o_ref = refs[0], refs[1:1 + nq], refs[1 + nq]
    ka = f_ref.shape[2]
    for s in range(segs):
        x = _strided_rows(x_refs, s, ka, segs) if strided else x_refs[0][s]
        o_ref[s] = _pack_pairs(_dot(f_ref[s], x.astype(BF16)))


def _fft_stage1(x4, part, g1, strided, cb=512):
    if strided:
        _, ka, b, c = x4.shape
    else:
        _, b, ka, c = x4.shape
    rows = g1.shape[1]
    segs = SUBLANES
    cb = min(cb, c)
    if strided:
        nq = cb // LANES
        x_specs = [pl.BlockSpec((None, ka, segs, LANES), functools.partial(lambda j, jc, q: (part, 0, j, jc * nq + q), q=q))
                   for q in range(nq)]
    else:
        nq = 1
        x_specs = [pl.BlockSpec((None, segs, ka, cb), lambda j, jc: (part, j, 0, jc))]
    return pl.pallas_call(
        functools.partial(_fft_stage1_kernel, segs=segs, nq=nq, strided=strided),
        grid=(b // segs, c // cb),
        in_specs=[pl.BlockSpec((segs, rows, ka), lambda j, jc: (j, 0, 0))] + x_specs,
        out_specs=pl.BlockSpec((segs, rows // 2, cb), lambda j, jc: (j, 0, jc)),
        out_shape=_sds((b, rows // 2, c), U32),
        compiler_params=_params("parallel", "parallel"),
        name="fft_stage1",
    )(g1, *([x4] * nq))


def _fft_filter_stage1_kernel(z_ref, w1_ref, b1_ref, w2_ref, b2_ref, w3_ref, b3_ref, fr_ref, w4_ref, dl_ref, f_ref,
                              o_ref, h_ref, *, seq, segs):
    a_n = z_ref.shape[1]
    inv = 1.0 / (seq - 1)

    @pl.when(pl.program_id(1) == 0)
    def _():
        for s in range(segs):
            h = jnp.sin(fr_ref[0:1, :] * (_dot(z_ref[s], w1_ref[...]) + b1_ref[...]))
            h = jnp.sin(fr_ref[1:2, :] * (_dot(h.astype(BF16), w2_ref[...]) + b2_ref[...]))
            h = jnp.sin(fr_ref[2:3, :] * (_dot(h.astype(BF16), w3_ref[...]) + b3_ref[...]))
            h_ref[s] = h.astype(BF16)

    for s in range(segs):
        b = pl.program_id(0) * segs + s
        n = (lax.broadcasted_iota(jnp.int32, (a_n, 1), 0) * FFT_B + b).astype(F32)
        e_fwd = jnp.exp(-(n * inv) * dl_ref[...])
        e_rev = jnp.where(n == 0.0, 0.0, jnp.exp(-((seq - n) * inv) * dl_ref[...]))
        h = h_ref[s]
        for o in range(HY_ORDER):
            k = jnp.concatenate([_dot(h, w4_ref[2 * o]) * e_fwd, _dot(h, w4_ref[2 * o + 1]) * e_rev], axis=0)
            o_ref[o, s] = _pack_pairs(_dot(f_ref[s], k.astype(BF16)))


def _block_diag2(w):
    z = jnp.zeros_like(w)
    return jnp.concatenate([jnp.concatenate([w, z], axis=1), jnp.concatenate([z, w], axis=1)], axis=0)


def _fft_filter_stage1(l, c, w1, b1, w2, b2, w3, b3, freq, w4, g1, cb=256):
    fw = w1.shape[1]
    emb = 64
    a = l // FFT_B
    rows = g1.shape[1]
    segs = SUBLANES
    cb = min(cb, c)
    z = _hyena_pos_features(l)
    z_rev = jnp.concatenate([z[:1], z[:0:-1]], axis=0)
    pad = lambda t: jnp.pad(t, ((0, 0), (0, emb - HY_EMB)))
    z2 = jnp.concatenate([pad(z), pad(z_rev)], axis=1).astype(BF16)
    z2 = z2.reshape(a, FFT_B, 2 * emb).transpose(1, 0, 2)
    both = lambda v: jnp.concatenate([v, v]).reshape(1, 2 * fw)
    w1d = _block_diag2(jnp.pad(w1, ((0, emb - HY_EMB), (0, 0)))).astype(BF16)
    w4r = w4.reshape(fw, HY_ORDER * 2, c).transpose(1, 0, 2)
    zeros = jnp.zeros_like(w4r)
    fwd_dir = (jnp.arange(HY_ORDER * 2) % 2 == 0)[:, None, None]
    w4d = jnp.concatenate([jnp.where(fwd_dir, w4r, zeros), jnp.where(fwd_dir, zeros, w4r)], axis=1).astype(BF16)
    max_decay = math.log(HY_DECAY_TARGET) / HY_FAST_PCT
    min_decay = math.log(HY_DECAY_TARGET) / HY_SLOW_PCT
    deltas = jnp.abs(jnp.linspace(min_decay, max_decay, c, dtype=F32)).reshape(1, c)
    full = lambda shape: pl.BlockSpec(shape, lambda j, jc: (0,) * len(shape))
    return pl.pallas_call(
        functools.partial(_fft_filter_stage1_kernel, seq=l, segs=segs),
        grid=(FFT_B // segs, c // cb),
        in_specs=[
            pl.BlockSpec((segs, a, 2 * emb), lambda j, jc: (j, 0, 0)),
            full((2 * emb, 2 * fw)), full((1, 2 * fw)), full((2 * fw, 2 * fw)), full((1, 2 * fw)),
            full((2 * fw, 2 * fw)), full((1, 2 * fw)), full((3, 2 * fw)),
            pl.BlockSpec((HY_ORDER * 2, 2 * fw, cb), lambda j, jc: (0, 0, jc)),
            pl.BlockSpec((1, cb), lambda j, jc: (0, jc)),
            pl.BlockSpec((segs, rows, 2 * a), lambda j, jc: (j, 0, 0)),
        ],
        out_specs=pl.BlockSpec((HY_ORDER, segs, rows // 2, cb), lambda j, jc: (0, j, 0, jc)),
        out_shape=_sds((HY_ORDER, FFT_B, rows // 2, c), U32),
        scratch_shapes=[pltpu.VMEM((segs, a, 2 * fw), BF16)],
        compiler_params=_params("parallel", "arbitrary"),
        name="fft_filter_stage1",
    )(z2, w1d, both(b1), _block_diag2(w2).astype(BF16), both(b2), _block_diag2(w3).astype(BF16), both(b3),
      jnp.concatenate([freq, freq], axis=1), w4d, deltas, g1)


def _fft_mid_kernel(*refs, kq, nc):
    mf_ref, mi_ref = refs[0], refs[1]
    y_refs, k_refs = refs[2:2 + nc], refs[2 + nc:2 + 2 * nc]
    o_ref = refs[2 + 2 * nc]
    b = mf_ref.shape[0] // 2
    for q in range(kq):
        x = _dot(mf_ref[...], _unpack_pairs(_strided_rows(y_refs, q, b, kq)))
        k = _dot(mf_ref[...], _unpack_pairs(_strided_rows(k_refs, q, b, kq)))
        pr, pi = _cmul(x[:b], x[b:], k[:b], k[b:])
        o_ref[q] = _pack_pairs(_dot(mi_ref[...], jnp.concatenate([pr, pi], axis=0).astype(BF16)))


def _fft_mid(y3, yk4, order, m_fwd, m_inv):
    b, na, c = y3.shape
    kq = SUBLANES
    nc = c // LANES
    mat = pl.BlockSpec((2 * b, 2 * b), lambda i: (0, 0))
    y_specs = [pl.BlockSpec((b, kq, LANES), functools.partial(lambda i, q: (0, i, q), q=q)) for q in range(nc)]
    k_specs = [pl.BlockSpec((None, b, kq, LANES), functools.partial(lambda i, q: (order, 0, i, q), q=q))
               for q in range(nc)]
    return pl.pallas_call(
        functools.partial(_fft_mid_kernel, kq=kq, nc=nc),
        grid=(na // kq,),
        in_specs=[mat, mat] + y_specs + k_specs,
        out_specs=pl.BlockSpec((kq, b, c), lambda i: (i, 0, 0)),
        out_shape=_sds((na, b, c), U32),
        compiler_params=_params("parallel"),
        name="fft_mid",
    )(m_fwd, m_inv, *([y3] * nc), *([yk4] * nc))


def _fft_stage3_kernel(*refs, segs, nq, u_strided, head_dim):
    f_ref, v_refs = refs[0], refs[1:1 + nq]
    nu = nq if u_strided else 1
    u_refs, g_refs = refs[1 + nq:1 + nq + nu], refs[1 + nq + nu:1 + 2 * nq + nu]
    sk_ref, ng_ref, o_ref = refs[1 + 2 * nq + nu:]
    a = f_ref.shape[1]
    rows = f_ref.shape[2] // 2
    c = o_ref.shape[-1]
    for s in range(segs):
        y = _dot(f_ref[s], _unpack_pairs(_strided_rows(v_refs, s, rows, segs)))
        u = _strided_rows(u_refs, s, a, segs) if u_strided else u_refs[0][s]
        z = _strided_rows(g_refs, s, a, segs) * (y + u * sk_ref[...])
        if head_dim:
            heads = [z[:, h:h + head_dim] for h in range(0, c, head_dim)]
            heads = [zh * lax.rsqrt(jnp.mean(zh * zh, axis=-1, keepdims=True) + EPS) for zh in heads]
            z = jnp.concatenate(heads, axis=-1) * ng_ref[...]
        o_ref[s] = z


def _fft_stage3(v3, f3, u4, u_part, u_strided, g4, g_part, skip, norm_g, head_dim, cb=512):
    a = f3.shape[1]
    rows, b, c = v3.shape
    segs = SUBLANES
    cb = min(cb, c)
    nq = cb // LANES
    chunk = lambda shape, idx: [pl.BlockSpec(shape, functools.partial(idx, q=q)) for q in range(nq)]
    v_specs = chunk((rows, segs, LANES), lambda j, jc, q: (0, j, jc * nq + q))
    nat = lambda part: chunk((None, a, segs, LANES), lambda j, jc, q: (part, 0, j, jc * nq + q))
    u_specs = nat(u_part) if u_strided else [pl.BlockSpec((None, segs, a, cb), lambda j, jc: (u_part, j, 0, jc))]
    vec = pl.BlockSpec((1, cb), lambda j, jc: (0, jc))
    return pl.pallas_call(
        functools.partial(_fft_stage3_kernel, segs=segs, nq=nq, u_strided=u_strided, head_dim=head_dim),
        grid=(b // segs, c // cb),
        in_specs=[pl.BlockSpec((segs, a, 2 * rows), lambda j, jc: (j, 0, 0))] + v_specs + u_specs + nat(g_part)
        + [vec, vec],
        out_specs=pl.BlockSpec((segs, a, cb), lambda j, jc: (j, 0, jc)),
        out_shape=_sds((b, a, c), F32),
        compiler_params=_params("parallel", "parallel"),
        name="fft_stage3",
    )(f3, *([v3] * nq), *([u4] * (nq if u_strided else 1)), *([g4] * nq), skip.reshape(1, c), norm_g.reshape(1, c))


def _hyena_mixer(proj, conv_w, conv_b, filt, skip, norm_g, c):
    l = proj.shape[0]
    a = l // FFT_B
    hyc4 = _dwconv1d(proj, conv_w, conv_b, 3, c).reshape(3, a, FFT_B, c)
    m_fwd, m_inv = _stage2_matrices()
    g1, f3 = _stage1_matrices(l)
    yk = _fft_filter_stage1(l, c, *filt, g1)
    z4, part, strided = hyc4, 0, True
    for o in range(HY_ORDER):
        y3 = _fft_stage1(z4, part, g1, strided)
        v3 = _fft_mid(y3, yk, o, m_fwd, m_inv)
        last = o == HY_ORDER - 1
        z = _fft_stage3(v3, f3, z4, part, strided, hyc4, 1 + o, skip[o], norm_g, c // HY_HEADS if last else 0)
        z4, part, strided = z[None], 0, False
    return z


def _lower_bound(lg_ref, slot):
    lg = lg_ref[...]
    e = jnp.exp(lg - jnp.max(lg, axis=0, keepdims=True))
    sm = e / jnp.sum(e, axis=0, keepdims=True)
    return jnp.sum(sm[:slot + 1], axis=0, keepdims=True)


def _prefix_sum_rows(x):
    t = x.shape[0]
    row = lax.broadcasted_iota(jnp.int32, x.shape, 0)
    s = 1
    while s < t:
        x = x + jnp.where(row >= s, pltpu.roll(x, s, 0), 0.0)
        s *= 2
    return x


def _gla_kernel(*refs, slot, reverse, nchunk, chunk, hpb, with_out, combine):
    it = iter(refs)
    i_ref, f_ref, lg_ref, s0_ref = next(it), next(it), next(it), next(it)
    q_ref = next(it) if with_out else None
    ofw_ref, g_ref, ng_ref = (next(it), next(it), next(it)) if combine else (None, None, None)
    o_ref = next(it) if with_out else None
    sf_ref, st_ref = next(it), next(it)
    ke = HG_EXPAND

    @pl.when(pl.program_id(1) == 0)
    def _():
        st_ref[...] = s0_ref[...]

    lb = _lower_bound(lg_ref, slot)
    row = lax.broadcasted_iota(jnp.int32, (chunk, chunk), 0)
    col = lax.broadcasted_iota(jnp.int32, (chunk, chunk), 1)
    keep = (col >= row) if reverse else (col <= row)
    nt = (((1,), (1,)), ((), ()))
    tn = (((0,), (0,)), ((), ()))
    st = [st_ref[h] for h in range(hpb)]
    order = range(nchunk - 1, -1, -1) if reverse else range(nchunk)
    for ci in order:
        rows = slice(ci * chunk, (ci + 1) * chunk)
        f = lb + (1.0 - lb) * jax.nn.sigmoid(f_ref[rows, :])
        logf = jnp.log(f)
        key = 1.0 - f
        b = _prefix_sum_rows(logf)
        total = b[chunk - 1:chunk, :]
        if reverse:
            b = total - b + logf
        v = i_ref[rows, :].astype(BF16)
        k_state = (key * jnp.exp(total - b)).astype(BF16)
        decay = jnp.exp(total)
        if with_out:
            q = q_ref[rows, :]
            q_in = (q * jax.nn.sigmoid(q) * jnp.exp(b)).astype(BF16)
            k_intra = (key * jnp.exp(-b)).astype(BF16)
        outs = []
        for h in range(hpb):
            hl = slice(h * ke, (h + 1) * ke)
            if with_out:
                att = lax.dot_general(q_in[:, hl], k_intra[:, hl], nt, preferred_element_type=F32)
                att = jnp.where(keep, att, 0.0).astype(BF16)
                o = _dot(att, v[:, hl]) + lax.dot_general(q_in[:, hl], st[h].astype(BF16), nt,
                                                           preferred_element_type=F32)
                if combine:
                    o = o + ofw_ref[rows, hl]
                    o = o * lax.rsqrt(jnp.mean(o * o, axis=-1, keepdims=True) + EPS)
                outs.append(o)
            st[h] = st[h] * decay[:, hl] + lax.dot_general(v[:, hl], k_state[:, hl], tn, preferred_element_type=F32)
        if with_out:
            o = jnp.concatenate(outs, axis=-1)
            if combine:
                g = g_ref[rows, :]
                o = o * ng_ref[...] * (g * jax.nn.sigmoid(g))
            o_ref[rows, :] = o.astype(o_ref.dtype)
    for h in range(hpb):
        st_ref[h] = st[h]

    @pl.when(pl.program_id(1) == pl.num_programs(1) - 1)
    def _():
        for h in range(hpb):
            sf_ref[h] = st[h]


def _gla(src, cols, lb_logits, slot, s0, reverse, with_out, fwd_out=None, norm_g=None, out_dtype=F32, tb=512, hpb=8):
    l = src.shape[0]
    heads = s0.shape[0]
    ke = HG_EXPAND
    tb = min(tb, l)
    nblk = l // tb
    while heads % hpb or any(off % hpb for off in cols.values()):
        hpb //= 2
    wd = hpb * ke
    combine = fwd_out is not None
    blk = (lambda j: nblk - 1 - j) if reverse else (lambda j: j)
    col = lambda name: pl.BlockSpec((tb, wd), lambda h, j: (blk(j), cols[name] // hpb + h))
    head_rows = pl.BlockSpec((tb, wd), lambda h, j: (blk(j), h))
    state = pl.BlockSpec((hpb, ke, ke), lambda h, j: (h, 0, 0))
    in_specs = [col("i"), col("f"), pl.BlockSpec((lb_logits.shape[0], wd), lambda h, j: (0, h)), state]
    args = [src, src, lb_logits, s0]
    out_specs, out_shape = [], []
    if with_out:
        in_specs.append(col("q"))
        args.append(src)
        out_specs.append(head_rows)
        out_shape.append(_sds((l, heads * ke), out_dtype))
    if combine:
        in_specs += [head_rows, col("g"), pl.BlockSpec((1, wd), lambda h, j: (0, h))]
        args += [fwd_out, src, norm_g.reshape(1, heads * ke)]
    out_specs.append(state)
    out_shape.append(_sds((heads, ke, ke), F32))
    res = pl.pallas_call(
        functools.partial(_gla_kernel, slot=slot, reverse=reverse, nchunk=tb // HG_CHUNK, chunk=HG_CHUNK, hpb=hpb,
                          with_out=with_out, combine=combine),
        grid=(heads // hpb, nblk),
        in_specs=in_specs,
        out_specs=out_specs,
        out_shape=out_shape,
        scratch_shapes=[pltpu.VMEM((hpb, ke, ke), F32)],
        compiler_params=_params("parallel", "arbitrary"),
        name="hgrn_gla",
    )(*args)
    return res if with_out else (None, res[0])


def _hgrn_mixer(proj, q0, ctx_proj, lb_logits, slot, norm_g, heads):
    qb = q0 // HG_EXPAND
    zeros = jnp.zeros((heads, HG_EXPAND, HG_EXPAND), F32)
    _, s0_fw = _gla(ctx_proj, {"i": 0, "f": heads}, lb_logits, slot, zeros, False, False)
    _, s0_bw = _gla(ctx_proj, {"i": 0, "f": 2 * heads}, lb_logits, slot, zeros, True, False)
    cols = {"q": qb, "i": qb + heads, "f": qb + 2 * heads, "g": qb + 4 * heads}
    o_fw, _ = _gla(proj, cols, lb_logits, slot, s0_fw, False, True)
    cols["f"] = qb + 3 * heads
    y, _ = _gla(proj, cols, lb_logits, slot, s0_bw, True, True, fwd_out=o_fw, norm_g=norm_g, out_dtype=BF16)
    return y


def _outproj_kernel(*refs, na, nc):
    a_refs = refs[:nc]
    b_ref, w_ref, x_ref, g_ref, o_ref = refs[nc:]
    rows = a_refs[0].shape[0]
    ya =jnp.concatenate([_strided_rows(a_refs, s, rows, na) for s in range(na)], axis=0).astype(BF16)
    y = _dot(jnp.concatenate([ya, b_ref[...]], axis=1), w_ref[...])
    o_ref[...] = x_ref[...] + g_ref[...] * y


def _outproj(ya3, yb, w, x, gate, tn=1024):
    m, d = x.shape
    b, a, k1 = ya3.shape
    k2 = yb.shape[1]
    na = min(SUBLANES, a)
    tm = na * b
    tn = min(tn, d)
    nc = k1 // LANES
    a_specs = [pl.BlockSpec((b, na, LANES), functools.partial(lambda j, i, q: (0, i, q), q=q)) for q in range(nc)]
    return pl.pallas_call(
        functools.partial(_outproj_kernel, na=na, nc=nc),
        grid=(d // tn, m // tm),
        in_specs=a_specs + [
            pl.BlockSpec((tm, k2), lambda j, i: (i, 0)),
            pl.BlockSpec((k1 + k2, tn), lambda j, i: (0, j)),
            pl.BlockSpec((tm, tn), lambda j, i: (i, j)),
            pl.BlockSpec((1, tn), lambda j, i: (0, j)),
        ],
        out_specs=pl.BlockSpec((tm, tn), lambda j, i: (i, j)),
        out_shape=_sds((m, d), F32),
        compiler_params=_params("parallel", "parallel"),
        name="outproj_residual",
    )(*([ya3] * nc), yb, w, x, gate)


def _ffn_kernel(h_hbm, wa_ref, wu_ref, cw_ref, cb_ref, wd_ref, x_hbm, g_ref, fg_ref, o_ref,
                hext_ref, x_ref, act_ref, sem, *, gw, final):
    i, j = pl.program_id(0), pl.program_id(1)
    ni, nj = pl.num_programs(0), pl.num_programs(1) - 1
    tm = o_ref.shape[0]
    n = tm + 2 * gw
    row0 = i * tm

    def h_copy(src_row, dst_row, rows, k):
        return pltpu.make_async_copy(h_hbm.at[pl.ds(src_row, rows)], hext_ref.at[pl.ds(dst_row, rows)], sem.at[k])

    x_copy = pltpu.make_async_copy(x_hbm.at[pl.ds(row0, tm)], x_ref, sem.at[3])

    def h_block(blk, start):
        r0 = blk * tm
        act = (lambda c: c.start()) if start else (lambda c: c.wait())
        act(h_copy(r0, gw, tm, 0))

        @pl.when(blk > 0)
        def _():
            act(h_copy(r0 - gw, 0, gw, 1))

        @pl.when(blk < ni - 1)
        def _():
            act(h_copy(r0 + tm, gw + tm, gw, 2))

        if start:
            @pl.when(blk == 0)
            def _():
                hext_ref[0:gw, :] = jnp.zeros((gw, hext_ref.shape[1]), hext_ref.dtype)

            @pl.when(blk == ni - 1)
            def _():
                hext_ref[gw + tm:n, :] = jnp.zeros((gw, hext_ref.shape[1]), hext_ref.dtype)

    def up_part():
        a = _dot(hext_ref[...], wa_ref[...])
        u = _dot(hext_ref[gw:gw + tm, :], wu_ref[...])
        col = lax.broadcasted_iota(jnp.int32, a.shape, 0) % gw
        a_m1 = jnp.where(col == 0, 0.0, pltpu.roll(a, 1, 0))
        a_p1 = jnp.where(col == gw - 1, 0.0, pltpu.roll(a, n - 1, 0))
        conv = cb_ref[...]
        for di in range(3):
            rows = slice(di * gw, di * gw + tm)
            conv = conv + (a_m1[rows] * cw_ref[3 * di:3 * di + 1, :] + a[rows] * cw_ref[3 * di + 1:3 * di + 2, :]
                           + a_p1[rows] * cw_ref[3 * di + 2:3 * di + 3, :])
        gelu = 0.5 * conv * (1.0 + lax.erf(conv * (1.0 / math.sqrt(2.0))))
        act_ref[j % 2] = (gelu * u).astype(BF16)

    def down_part():
        o_ref[...] += _dot(act_ref[(j + 1) % 2], wd_ref[...])

    @pl.when(j == 0)
    def _():
        @pl.when(i == 0)
        def _():
            h_block(i, start=True)

        x_copy.start()
        o_ref[...] = jnp.zeros_like(o_ref)
        h_block(i, start=False)
        up_part()

    @pl.when(jnp.logical_and(j > 0, j < nj))
    def _():
        down_part()
        up_part()

    @pl.when(j == nj)
    def _():
        @pl.when(i < ni - 1)
        def _():
            h_block(i + 1, start=True)

        down_part()
        x_copy.wait()
        y = x_ref[...] + g_ref[...] * o_ref[...]
        if final:
            y = y * lax.rsqrt(jnp.mean(y * y, axis=-1, keepdims=True) + EPS) * fg_ref[...]
        o_ref[...] = y


def _conv_ffn(h, x, gate, w_up, w_down, layer, conv_w, conv_b, final_g, gw, tf=512):
    l, d = x.shape
    dff = w_down.shape[1]
    tm, tf = min(FFN_ROWS, l), min(tf, dff)
    nj = dff // tf
    final = final_g is not None
    fg = (final_g if final else jnp.ones((d,), F32)).reshape(1, d)
    hbm = pl.BlockSpec(memory_space=pl.ANY)
    up = lambda j: jnp.minimum(j, nj - 1)
    down = lambda j: jnp.maximum(j - 1, 0)
    return pl.pallas_call(
        functools.partial(_ffn_kernel, gw=gw, final=final),
        grid=(l // tm, nj + 1),
        in_specs=[
            hbm,
            pl.BlockSpec((None, d, tf), lambda i, j: (layer, 0, up(j))),
            pl.BlockSpec((None, d, tf), lambda i, j: (layer, 0, nj + up(j))),
            pl.BlockSpec((9, tf), lambda i, j: (0, up(j))),
            pl.BlockSpec((1, tf), lambda i, j: (0, up(j))),
            pl.BlockSpec((None, tf, d), lambda i, j: (layer, down(j), 0)),
            hbm,
            pl.BlockSpec((1, d), lambda i, j: (0, 0)),
            pl.BlockSpec((1, d), lambda i, j: (0, 0)),
        ],
        out_specs=pl.BlockSpec((tm, d), lambda i, j: (i, 0)),
        out_shape=_sds((l, d), F32),
        scratch_shapes=[pltpu.VMEM((tm + 2 * gw, d), BF16), pltpu.VMEM((tm, d), F32),
                        pltpu.VMEM((2, tm, tf), BF16), pltpu.SemaphoreType.DMA((4,))],
        compiler_params=_params("arbitrary", "arbitrary"),
        name="conv_ffn",
    )(h, w_up, w_up, conv_w.reshape(9, dff), conv_b.reshape(1, dff), w_down, x, gate, fg)


def _pool_kernel(xm_ref, xp_ref, xn_ref, ng_ref, sc_ref, sh_ref, w_ref, b_ref, ps_ref, g_ref, ng2_ref, sc2_ref,
                 sh2_ref, o_ref, h_ref, *, seq, windows):
    i = pl.program_id(0)
    tm = xm_ref.shape[0]
    pd = w_ref.shape[1]
    norm = lambda x: _rms_mod(x, ng_ref[...], sc_ref[...], sh_ref[...])
    xm = xm_ref[...]
    hm = norm(xm)
    hp = jnp.where(i > 0, norm(xp_ref[...]), 0.0)
    hn = jnp.where(i < pl.num_programs(0) - 1, norm(xn_ref[...]), 0.0)
    t = i * tm + lax.broadcasted_iota(jnp.int32, (tm, 1), 0)
    n = tm + 2 * POOL_HALO
    for gi, win in enumerate(windows):
        cols = slice(gi * pd, (gi + 1) * pd)
        e = jnp.concatenate([hp[:, cols], hm[:, cols], hn[:, cols]], axis=0)
        p = e + pltpu.roll(e, 1, 0)
        w = 2
        while w < win:
            p = pltpu.roll(p, w // 2, 0) + pltpu.roll(p, n - w // 2, 0)
            w *= 2
        count = (jnp.clip(t + win // 2, 0, seq) - jnp.clip(t - win // 2, 0, seq)).astype(F32)
        mean = p[POOL_HALO:POOL_HALO + tm] / count
        y = _dot((mean - hm[:, cols]).astype(BF16), w_ref[gi]) + b_ref[gi]
        o_ref[:, cols] = xm[:, cols] + g_ref[:, cols] * (y * ps_ref[:, cols])
    h_ref[...] = _rms_mod(o_ref[...], ng2_ref[...], sc2_ref[...], sh2_ref[...]).astype(h_ref.dtype)


def _pool_layer(x, norm_g, sc, sh, w, b, scale, gate, norm2_g, sc2, sh2, tm=512):
    l, d = x.shape
    ng, pd, _ = w.shape
    assert max(POOL_WINDOWS) // 2 <= POOL_HALO == SUBLANES
    tm = min(tm, l)
    r = tm // SUBLANES
    last = l // SUBLANES - 1
    row = pl.BlockSpec((1, d), lambda i: (0, 0))
    return pl.pallas_call(
        functools.partial(_pool_kernel, seq=l, windows=POOL_WINDOWS),
        grid=(l // tm,),
        in_specs=[
            pl.BlockSpec((tm, d), lambda i: (i, 0)),
            pl.BlockSpec((SUBLANES, d), lambda i: (jnp.maximum(i * r - 1, 0), 0)),
            pl.BlockSpec((SUBLANES, d), lambda i: (jnp.minimum((i + 1) * r, last), 0)),
            row, row, row,
            pl.BlockSpec((ng, pd, pd), lambda i: (0, 0, 0)),
            pl.BlockSpec((ng, 1, pd), lambda i: (0, 0, 0)),
            row, row, row, row, row,
        ],
        out_specs=[pl.BlockSpec((tm, d), lambda i: (i, 0)), pl.BlockSpec((tm, d), lambda i: (i, 0))],
        out_shape=[_sds((l, d), F32), _sds((l, d), BF16)],
        compiler_params=_params("parallel"),
        name="pool_mixer_residual",
    )(x, x, x, norm_g.reshape(1, d), sc, sh, w.astype(BF16), b.reshape(ng, 1, pd), scale.reshape(1, d), gate,
      norm2_g.reshape(1, d), sc2, sh2)


def kernel(x, c, ctx, c_ctx, norm_mix_g, norm_ffn_g, mod_w, mod_b, in_w, in_b, hy_conv_w, hy_conv_b, hy_w1, hy_b1, hy_w2, hy_b2, hy_w3, hy_b3, hy_freq, hy_w4, hy_skip, hy_norm_g, hg_lb_logits, hg_norm_g, out_w, pool_w, pool_b, pool_scale, ffn_up_w, ffn_conv_w, ffn_conv_b, ffn_down_w, final_norm_g):
    batch, _, d = x.shape
    depth = mod_w.shape[0]
    hy_d = hy_norm_g.shape[1]
    hg_d = hg_norm_g.shape[1]
    hg_q0 = 3 * hy_d
    hg_i0 = hg_q0 + hg_d
    hg_heads = hg_d // HG_EXPAND
    ffn_up_bf, ffn_down_bf = ffn_up_w.astype(BF16), ffn_down_w.astype(BF16)
    outs = []
    for bi in range(batch):
        xs = x[bi]
        cvecs = jnp.zeros((SUBLANES, d), F32).at[0].set(c[bi]).at[1].set(c_ctx)
        for l in range(depth):
            mod = _adaln(cvecs, mod_w, mod_b, l)
            sh1, sc1, g1, sh2, sc2, g2 = [mod[0:1, k * d:(k + 1) * d] for k in range(N_MOD)]
            if l % 2 == 0:
                e = l // 2
                in_w_bf = in_w[e].astype(BF16)
                proj = _norm_matmul(xs, norm_mix_g[l], sc1, sh1, in_w_bf, in_b[e])
                ctx_proj = _norm_matmul(ctx[bi], norm_mix_g[l], mod[1:2, d:2 * d], mod[1:2, 0:d],
                                        in_w_bf[:, hg_i0:hg_i0 + 3 * hg_d], in_b[e, hg_i0:hg_i0 + 3 * hg_d])
                filt = (hy_w1[e], hy_b1[e], hy_w2[e], hy_b2[e], hy_w3[e], hy_b3[e], hy_freq[e], hy_w4[e])
                y_hy = _hyena_mixer(proj, hy_conv_w[e], hy_conv_b[e], filt, hy_skip[e], hy_norm_g[e], hy_d)
                y_hg = _hgrn_mixer(proj, hg_q0, ctx_proj, hg_lb_logits, e, hg_norm_g[e], hg_heads)
                xs = _outproj(y_hy, y_hg, out_w[e].astype(BF16), xs, g1)
            else:
                od = l // 2
                xs, h = _pool_layer(xs, norm_mix_g[l], sc1, sh1, pool_w[od], pool_b[od], pool_scale[od], g1,
                                    norm_ffn_g[l], sc2, sh2)
            if l % 2 == 0:
                h = _norm_mod(xs, norm_ffn_g[l], sc2, sh2)
            xs = _conv_ffn(h, xs, g2, ffn_up_bf, ffn_down_bf, l, ffn_conv_w[l], ffn_conv_b[l],
                           final_norm_g if l == depth - 1 else None, GRID_W)
        outs.append(xs[None])
    return outs[0] if batch == 1 else jnp.concatenate(outs, axis=0)
```
